```python
import math
import jax, jax.numpy as jnp
from jax import lax
import numpy as np

D_MODEL = 1024
BATCH = 4
SEQ = 4096
DEPTH = 1
DEC_BATCH = 32
DEC_SEQ = 8
PAST_LEN = 16384
PAGE_SIZE = 128

SSM_WIDTH = D_MODEL // 2
SSM_CH = 16
SSM_GROUPS = SSM_WIDTH // SSM_CH
SSM_STATE = 64
ATTN_WIDTH = D_MODEL - SSM_WIDTH
HEAD_DIM = 64
N_HEADS = ATTN_WIDTH // HEAD_DIM
N_KV = 2
GQA_REP = N_HEADS // N_KV
KV_WIDTH = N_KV * HEAD_DIM
ROPE_DIM = HEAD_DIM // 4
ROPE_THETA = 500000.0
CMP_STRIDE = 16
CMP_BLOCK = 2 * CMP_STRIDE
CMP_HIDDEN = 2 * HEAD_DIM
SEL_BLOCK = 64
SEL_TOPN = 16
CMP_PER_SEL = SEL_BLOCK // CMP_STRIDE
FORCE_BONUS = 1000.0
WINDOW = 512
Q_BLOCK = 128
PLE_DIM = 256
PEER_HEADS = 8
PEER_KEYS = 128
PEER_EXPERTS = PEER_KEYS * PEER_KEYS
PEER_QDIM = 256
PEER_TOPK = 16
PEER_CHUNK = 256
RMS_EPS = 1e-6
NEG = -1e30
IN_COLS = SSM_WIDTH + ATTN_WIDTH + 6 * KV_WIDTH + 3 * N_HEADS

kernel_name = 'hymba_s5_nsa_peer_decode_step'


def rms_norm(x, g):
    xf = x.astype(jnp.float32)
    y = xf * lax.rsqrt(jnp.mean(xf * xf, axis=-1, keepdims=True) + RMS_EPS)
    return (y * g.astype(jnp.float32)).astype(x.dtype)


def rope_partial(x, pos):
    half = ROPE_DIM // 2
    inv = jnp.power(ROPE_THETA, -jnp.arange(half, dtype=jnp.float32) / half)
    ang = pos.astype(jnp.float32)[:, None] * inv[None, :]
    cos = jnp.cos(ang)[:, None, :]
    sin = jnp.sin(ang)[:, None, :]
    xf = x.astype(jnp.float32)
    x1 = xf[..., :half]
    x2 = xf[..., half:ROPE_DIM]
    out = jnp.concatenate([x1 * cos - x2 * sin, x2 * cos + x1 * sin, xf[..., ROPE_DIM:]], axis=-1)
    return out.astype(x.dtype)


def masked_softmax(s, mask):
    s = jnp.where(mask, s.astype(jnp.float32), NEG)
    p = jax.nn.softmax(s, axis=-1)
    return jnp.where(mask, p, 0.0)


def project(x, pos, lw):
    bsz, t, _ = x.shape
    z = rms_norm(x, lw['g_mix']) @ lw['w_in']
    cuts = np.cumsum([SSM_WIDTH, ATTN_WIDTH] + [KV_WIDTH] * 6).tolist()
    u, q, kc, vc, ks, vs, kw, vw, gl = jnp.split(z, cuts, axis=-1)
    heads = lambda a: a.reshape(bsz, t, N_KV, HEAD_DIM)
    q = rope_partial(rms_norm(q.reshape(bsz, t, N_HEADS, HEAD_DIM), lw['g_q']), pos)
    ks = rope_partial(rms_norm(heads(ks), lw['g_k_sel']), pos)
    kw = rope_partial(rms_norm(heads(kw), lw['g_k_win']), pos)
    gates = jax.nn.sigmoid(gl.astype(jnp.float32)).reshape(bsz, t, 3, N_HEADS)
    return u, q, heads(kc), heads(vc), ks, heads(vs), kw, heads(vw), gates


def _complex_affine_combine(e1, e2):
    a1r, a1i, b1r, b1i = e1
    a2r, a2i, b2r, b2i = e2
    return (a1r * a2r - a1i * a2i, a1r * a2i + a1i * a2r,
            a2r * b1r - a2i * b1i + b2r, a2r * b1i + a2i * b1r + b2i)


def ssm_mixer(u, s0_re, s0_im, lw):
    f32 = jnp.float32
    bsz, t, _ = u.shape
    a_re = lw['ssm_a_re'].astype(f32)
    a_im = lw['ssm_a_im'].astype(f32)
    dt = jnp.exp(lw['ssm_log_dt'].astype(f32))[:, None]
    mag = jnp.exp(a_re * dt)
    ab_re = mag * jnp.cos(a_im * dt)
    ab_im = mag * jnp.sin(a_im * dt)
    den = a_re * a_re + a_im * a_im
    f_re = ((ab_re - 1.0) * a_re + ab_im * a_im) / den
    f_im = (ab_im * a_re - (ab_re - 1.0) * a_im) / den
    b_re = lw['ssm_b_re'].astype(f32)
    b_im = lw['ssm_b_im'].astype(f32)
    bb_re = f_re[..., None] * b_re - f_im[..., None] * b_im
    bb_im = f_re[..., None] * b_im + f_im[..., None] * b_re
    uf = u.astype(f32).reshape(bsz, t, SSM_GROUPS, SSM_CH)
    bu_re = jnp.einsum('btgc,gpc->btgp', uf, bb_re)
    bu_im = jnp.einsum('btgc,gpc->btgp', uf, bb_im)
    a_t_re = jnp.broadcast_to(ab_re, bu_re.shape)
    a_t_im = jnp.broadcast_to(ab_im, bu_im.shape)
    pr, pim, xr, xi = lax.associative_scan(_complex_affine_combine, (a_t_re, a_t_im, bu_re, bu_im), axis=1)
    s0r = s0_re.astype(f32)[:, None]
    s0i = s0_im.astype(f32)[:, None]
    xr = xr + pr * s0r - pim * s0i
    xi = xi + pr * s0i + pim * s0r
    c_re = lw['ssm_c_re'].astype(f32)
    c_im = lw['ssm_c_im'].astype(f32)
    y = jnp.einsum('btgp,gcp->btgc', xr, c_re) - jnp.einsum('btgp,gcp->btgc', xi, c_im)
    y = y.reshape(bsz, t, SSM_WIDTH) + lw['ssm_d'].astype(f32) * u.astype(f32)
    zg = jax.nn.gelu(y)
    out = zg * jax.nn.sigmoid(zg @ lw['ssm_w_glu'].astype(f32) + lw['ssm_b_glu'].astype(f32))
    return out.astype(u.dtype), xr[:, -1], xi[:, -1]


def pad_rows(rows, mult):
    extra = (-rows.shape[1]) % mult
    return jnp.pad(rows, ((0, 0), (0, extra), (0, 0), (0, 0)))


def compress_rows(rows, pe, w1, w2):
    bsz, t, g, hd = rows.shape
    nblk = t // CMP_STRIDE
    half = CMP_STRIDE * hd
    blk = rows.reshape(bsz, nblk, CMP_STRIDE, g, hd).transpose(0, 1, 3, 2, 4).reshape(bsz, nblk, g, half)
    h = jnp.einsum('bngk,skh->bngsh', blk, w1.reshape(2, half, CMP_HIDDEN))
    pre = h[:, :-1, :, 0] + h[:, 1:, :, 1] + pe.reshape(-1) @ w1
    return jax.nn.gelu(pre) @ w2


def compressed_kv(k_rows, v_rows, lw):
    kc = compress_rows(k_rows, lw['cmp_pe_k'], lw['cmp_w1_k'], lw['cmp_w2_k'])
    vc = compress_rows(v_rows, lw['cmp_pe_v'], lw['cmp_w1_v'], lw['cmp_w2_v'])
    kc_end = jnp.arange(kc.shape[1], dtype=jnp.int32) * CMP_STRIDE + (CMP_BLOCK - 1)
    kc = rope_partial(rms_norm(kc, lw['g_k_cmp']), kc_end)
    return kc, vc, kc_end


def sel_blocks(rows):
    bsz, t, g, hd = rows.shape
    return rows.reshape(bsz, t // SEL_BLOCK, SEL_BLOCK, g, hd).transpose(0, 3, 1, 2, 4)


def nsa_block(q, qpos, gates, kc, vc, kc_end, ks_bt, vs_bt, kw, vw, kwpos, n_sel):
    f32 = jnp.float32
    bsz, nq = q.shape[0], q.shape[1]
    qg = q.astype(f32).reshape(bsz, nq, N_KV, GQA_REP, HEAD_DIM) * (HEAD_DIM ** -0.5)
    s = jnp.einsum('bqgrd,bcgd->bqgrc', qg, kc.astype(f32))
    cmask = (kc_end[None, :] <= qpos[:, None])[None, :, None, None, :]
    p_cmp = masked_softmax(s, cmask)
    o_cmp = jnp.einsum('bqgrc,bcgd->bqgrd', p_cmp, vc.astype(f32))
    ns = ks_bt.shape[2]
    nc = kc.shape[1]
    imp = jnp.pad(p_cmp.sum(axis=3), ((0, 0), (0, 0), (0, 0), (0, ns * CMP_PER_SEL - nc)))
    imp = imp.reshape(bsz, nq, N_KV, ns, CMP_PER_SEL).sum(axis=-1)
    blk = jnp.arange(ns, dtype=jnp.int32)[None, :]
    cur = (qpos // SEL_BLOCK)[:, None]
    forced = ((blk == 0) | (blk == cur) | (blk == cur - 1))[None, :, None, :]
    future = (blk > cur)[None, :, None, :]
    score = jnp.where(future, NEG, jnp.where(forced, imp + FORCE_BONUS, imp))
    _, sel = lax.top_k(score, n_sel)
    b_ix = jnp.arange(bsz)[:, None, None, None]
    g_ix = jnp.arange(N_KV)[None, None, :, None]
    k_sel = ks_bt[b_ix, g_ix, sel].reshape(bsz, nq, N_KV, n_sel * SEL_BLOCK, HEAD_DIM).astype(f32)
    v_sel = vs_bt[b_ix, g_ix, sel].reshape(bsz, nq, N_KV, n_sel * SEL_BLOCK, HEAD_DIM).astype(f32)
    kpos = (sel[..., None] * SEL_BLOCK + jnp.arange(SEL_BLOCK, dtype=jnp.int32)).reshape(bsz, nq, N_KV, n_sel * SEL_BLOCK)
    smask = (kpos <= qpos[None, :, None, None])[:, :, :, None, :]
    s = jnp.einsum('bqgrd,bqgkd->bqgrk', qg, k_sel)
    o_sel = jnp.einsum('bqgrk,bqgkd->bqgrd', masked_softmax(s, smask), v_sel)
    s = jnp.einsum('bqgrd,bkgd->bqgrk', qg, kw.astype(f32))
    diff = qpos[:, None] - kwpos[None, :]
    wmask = ((diff >= 0) & (diff < WINDOW) & (kwpos[None, :] >= 0))[None, :, None, None, :]
    o_win = jnp.einsum('bqgrk,bkgd->bqgrd', masked_softmax(s, wmask), vw.astype(f32))
    g = gates.reshape(bsz, nq, 3, N_KV, GQA_REP)[..., None]
    o = g[:, :, 0] * o_cmp + g[:, :, 1] * o_sel + g[:, :, 2] * o_win
    return o.reshape(bsz, nq, ATTN_WIDTH)


def attention_prompt(q, kc_rows, vc_rows, ks, vs, kw, vw, gates, lw):
    bsz, t = q.shape[0], q.shape[1]
    kc, vc, kc_end = compressed_kv(pad_rows(kc_rows, SEL_BLOCK), pad_rows(vc_rows, SEL_BLOCK), lw)
    ks_bt = sel_blocks(pad_rows(ks, SEL_BLOCK))
    vs_bt = sel_blocks(pad_rows(vs, SEL_BLOCK))
    n_sel = min(SEL_TOPN, ks_bt.shape[2])
    nb = t // Q_BLOCK
    wb = WINDOW // Q_BLOCK

    def band(rows):
        p = jnp.pad(rows, ((0, 0), (WINDOW, 0), (0, 0), (0, 0))).reshape(bsz, nb + wb, Q_BLOCK, N_KV, HEAD_DIM)
        return jnp.concatenate([p[:, m:m + nb] for m in range(wb + 1)], axis=2).swapaxes(0, 1)

    kwpos = (jnp.arange(nb, dtype=jnp.int32) * Q_BLOCK - WINDOW)[:, None] + jnp.arange((wb + 1) * Q_BLOCK, dtype=jnp.int32)[None, :]
    blocks = lambda a: a.reshape(bsz, nb, Q_BLOCK, *a.shape[2:]).swapaxes(0, 1)
    qpos = jnp.arange(t, dtype=jnp.int32).reshape(nb, Q_BLOCK)

    def step(args):
        qb, qp, gb, kwb, vwb, kp = args
        return nsa_block(qb, qp, gb, kc, vc, kc_end, ks_bt, vs_bt, kwb, vwb, kp, n_sel)

    o = lax.map(step, (blocks(q), qpos, blocks(gates), band(kw), band(vw), kwpos))
    return o.swapaxes(0, 1).reshape(bsz, t, ATTN_WIDTH)


def gather_pages(pool, page_table):
    g = pool[page_table]
    return g.reshape(g.shape[0], g.shape[1] * g.shape[2], *pool.shape[2:])


def attention_sample(q, pos, kc_new, vc_new, ks_new, vs_new, kw_new, vw_new, gates,
                     past_kc, past_vc, past_ks, past_vs, buf_kw, buf_vw, past_len, lw):
    cat = lambda a, b: jnp.concatenate([a, b.astype(a.dtype)], axis=1)
    kc, vc, kc_end = compressed_kv(pad_rows(cat(past_kc, kc_new), SEL_BLOCK), pad_rows(cat(past_vc, vc_new), SEL_BLOCK), lw)
    ks_bt = sel_blocks(pad_rows(cat(past_ks, ks_new), SEL_BLOCK))
    vs_bt = sel_blocks(pad_rows(cat(past_vs, vs_new), SEL_BLOCK))
    n_sel = min(SEL_TOPN, ks_bt.shape[2])
    kw_all = cat(buf_kw, kw_new)
    vw_all = cat(buf_vw, vw_new)
    nbuf = buf_kw.shape[1]
    kwpos = past_len - nbuf + jnp.arange(kw_all.shape[1], dtype=jnp.int32)
    o = nsa_block(q, pos, gates, kc, vc, kc_end, ks_bt, vs_bt, kw_all, vw_all, kwpos, n_sel)
    keep = min(WINDOW, kw_all.shape[1])
    return o, kw_all[:, -keep:], vw_all[:, -keep:]


def peer_ffn(x, w_q, subkeys, u_tab, v_tab):
    shp = x.shape
    xt = x.reshape(-1, D_MODEL)
    n = xt.shape[0]
    q = (xt @ w_q).astype(jnp.float32).reshape(n, PEER_HEADS, 2, PEER_QDIM // 2)
    s = jnp.einsum('nhsd,hskd->nhsk', q, subkeys.astype(jnp.float32))
    v_top, i_top = lax.top_k(s, PEER_TOPK)
    cand = (v_top[:, :, 0, :, None] + v_top[:, :, 1, None, :]).reshape(n, PEER_HEADS, PEER_TOPK * PEER_TOPK)
    cand_idx = (i_top[:, :, 0, :, None] * PEER_KEYS + i_top[:, :, 1, None, :]).reshape(n, PEER_HEADS, PEER_TOPK * PEER_TOPK)
    vals, cpos = lax.top_k(cand, PEER_TOPK)
    eidx = jnp.take_along_axis(cand_idx, cpos, axis=-1)
    gate = jax.nn.softmax(vals, axis=-1)
    c = min(PEER_CHUNK, n)
    npad = (-n) % c
    xp = jnp.pad(xt, ((0, npad), (0, 0))).reshape(-1, c, D_MODEL)
    ep = jnp.pad(eidx, ((0, npad), (0, 0), (0, 0))).reshape(-1, c, PEER_HEADS, PEER_TOPK)
    gp = jnp.pad(gate, ((0, npad), (0, 0), (0, 0))).reshape(-1, c, PEER_HEADS, PEER_TOPK)

    def chunk(args):
        xc, ec, gc = args
        act = jax.nn.gelu(jnp.einsum('chkd,cd->chk', u_tab[ec].astype(jnp.float32), xc.astype(jnp.float32)))
        return jnp.einsum('chk,chkd->cd', gc * act, v_tab[ec].astype(jnp.float32))

    out = lax.map(chunk, (xp, ep, gp)).reshape(-1, D_MODEL)[:n]
    return out.reshape(shp).astype(x.dtype)


def finish_layer(x, ssm_y, attn_o, ple, lw):
    mixed = jnp.concatenate([rms_norm(ssm_y.astype(x.dtype), lw['g_out_ssm']),
                             rms_norm(attn_o.astype(x.dtype), lw['g_out_attn'])], axis=-1) @ lw['w_out']
    h = x + mixed
    h = h + peer_ffn(rms_norm(h, lw['g_ffn']), lw['peer_w_q'], lw['peer_subkeys'], lw['peer_u'], lw['peer_v'])
    gate = jax.nn.sigmoid(rms_norm(h, lw['g_ple']) @ lw['ple_w_gate'])
    return h + gate * (ple @ lw['ple_w_proj'])


def setup_inputs(seed: int = 0) -> dict:
    key = jax.random.key(seed)
    ks = jax.random.split(key, 64)
    ctr = [0]

    def nk():
        k = ks[ctr[0]]
        ctr[0] += 1
        return k

    def nrm(shape, scale):
        return scale * jax.random.normal(nk(), shape, jnp.float32)

    def gain(shape):
        return 1.0 + 0.05 * jax.random.normal(nk(), shape, jnp.float32)

    n_pages = PAST_LEN // PAGE_SIZE
    n_used = DEC_BATCH * n_pages
    n_pool = n_used + n_used // 4
    win_buf = min(WINDOW, PAST_LEN)
    pool_shape = (DEPTH, n_pool, PAGE_SIZE, N_KV, HEAD_DIM)
    win_shape = (DEPTH, DEC_BATCH, win_buf, N_KV, HEAD_DIM)
    st_shape = (DEPTH, DEC_BATCH, SSM_GROUPS, SSM_STATE)
    inp = {}
    inp['x_prompt'] = nrm((BATCH, SEQ, D_MODEL), 1.0)
    inp['x_sample'] = nrm((DEC_BATCH, DEC_SEQ, D_MODEL), 1.0)
    inp['cache_k_cmp'] = nrm(pool_shape, 1.0)
    inp['cache_v_cmp'] = nrm(pool_shape, 1.0)
    inp['cache_k_sel'] = nrm(pool_shape, 1.0)
    inp['cache_v_sel'] = nrm(pool_shape, 1.0)
    inp['cache_k_win'] = nrm(win_shape, 1.0)
    inp['cache_v_win'] = nrm(win_shape, 1.0)
    inp['state_ssm_re'] = nrm(st_shape, 0.1)
    inp['state_ssm_im'] = nrm(st_shape, 0.1)
    inp['page_table'] = jax.random.permutation(nk(), n_pool)[:n_used].reshape(DEC_BATCH, n_pages).astype(jnp.int32)
    inp['p_prompt'] = nrm((DEPTH, BATCH, SEQ, PLE_DIM), 1.0)
    inp['p_sample'] = nrm((DEPTH, DEC_BATCH, DEC_SEQ, PLE_DIM), 1.0)
    L = DEPTH
    inp['g_mix'] = gain((L, D_MODEL))
    inp['w_in'] = nrm((L, D_MODEL, IN_COLS), D_MODEL ** -0.5)
    inp['ssm_a_re'] = -0.5 + nrm((L, SSM_GROUPS, SSM_STATE), 0.01)
    inp['ssm_a_im'] = jnp.pi * jnp.arange(SSM_STATE, dtype=jnp.float32) + nrm((L, SSM_GROUPS, SSM_STATE), 0.01)
    inp['ssm_log_dt'] = jax.random.uniform(nk(), (L, SSM_GROUPS), jnp.float32, math.log(0.001), math.log(0.1))
    inp['ssm_b_re'] = nrm((L, SSM_GROUPS, SSM_STATE, SSM_CH), (0.5 / SSM_CH) ** 0.5)
    inp['ssm_b_im'] = nrm((L, SSM_GROUPS, SSM_STATE, SSM_CH), (0.5 / SSM_CH) ** 0.5)
    inp['ssm_c_re'] = nrm((L, SSM_GROUPS, SSM_CH, SSM_STATE), (0.5 / SSM_STATE) ** 0.5)
    inp['ssm_c_im'] = nrm((L, SSM_GROUPS, SSM_CH, SSM_STATE), (0.5 / SSM_STATE) ** 0.5)
    inp['ssm_d'] = nrm((L, SSM_WIDTH), 1.0)
    inp['ssm_w_glu'] = nrm((L, SSM_WIDTH, SSM_WIDTH), SSM_WIDTH ** -0.5)
    inp['ssm_b_glu'] = nrm((L, SSM_WIDTH), 0.01)
    inp['g_q'] = gain((L, HEAD_DIM))
    inp['g_k_cmp'] = gain((L, HEAD_DIM))
    inp['g_k_sel'] = gain((L, HEAD_DIM))
    inp['g_k_win'] = gain((L, HEAD_DIM))
    inp['cmp_pe_k'] = nrm((L, CMP_BLOCK, HEAD_DIM), 0.1)
    inp['cmp_w1_k'] = nrm((L, CMP_BLOCK * HEAD_DIM, CMP_HIDDEN), (CMP_BLOCK * HEAD_DIM) ** -0.5)
    inp['cmp_w2_k'] = nrm((L, CMP_HIDDEN, HEAD_DIM), CMP_HIDDEN ** -0.5)
    inp['cmp_pe_v'] = nrm((L, CMP_BLOCK, HEAD_DIM), 0.1)
    inp['cmp_w1_v'] = nrm((L, CMP_BLOCK * HEAD_DIM, CMP_HIDDEN), (CMP_BLOCK * HEAD_DIM) ** -0.5)
    inp['cmp_w2_v'] = nrm((L, CMP_HIDDEN, HEAD_DIM), CMP_HIDDEN ** -0.5)
    inp['g_out_ssm'] = gain((L, SSM_WIDTH))
    inp['g_out_attn'] = gain((L, ATTN_WIDTH))
    inp['w_out'] = nrm((L, D_MODEL, D_MODEL), D_MODEL ** -0.5)
    inp['g_ffn'] = gain((L, D_MODEL))
    inp['peer_w_q'] = nrm((L, D_MODEL, PEER_HEADS * PEER_QDIM), D_MODEL ** -0.5)
    inp['peer_subkeys'] = nrm((L, PEER_HEADS, 2, PEER_KEYS, PEER_QDIM // 2), (PEER_QDIM // 2) ** -0.5)
    inp['peer_u'] = nrm((L, PEER_EXPERTS, D_MODEL), D_MODEL ** -0.5)
    inp['peer_v'] = nrm((L, PEER_EXPERTS, D_MODEL), PEER_HEADS ** -0.5)
    inp['g_ple'] = gain((L, D_MODEL))
    inp['ple_w_gate'] = nrm((L, D_MODEL, D_MODEL), D_MODEL ** -0.5)
    inp['ple_w_proj'] = nrm((L, PLE_DIM, D_MODEL), PLE_DIM ** -0.5)
    return inp


def reference(x_prompt, x_sample, cache_k_cmp, cache_v_cmp, cache_k_sel, cache_v_sel, cache_k_win, cache_v_win,
              state_ssm_re, state_ssm_im, page_table, p_prompt, p_sample,
              g_mix, w_in, ssm_a_re, ssm_a_im, ssm_log_dt, ssm_b_re, ssm_b_im, ssm_c_re, ssm_c_im, ssm_d,
              ssm_w_glu, ssm_b_glu, g_q, g_k_cmp, g_k_sel, g_k_win, cmp_pe_k, cmp_w1_k, cmp_w2_k,
              cmp_pe_v, cmp_w1_v, cmp_w2_v, g_out_ssm, g_out_attn, w_out, g_ffn, peer_w_q, peer_subkeys,
              peer_u, peer_v, g_ple, ple_w_gate, ple_w_proj):
    bp, tp, _ = x_prompt.shape
    bs, ts, _ = x_sample.shape
    past_len = page_table.shape[1] * cache_k_cmp.shape[2]
    pos_p = jnp.arange(tp, dtype=jnp.int32)
    pos_s = past_len + jnp.arange(ts, dtype=jnp.int32)
    hp, hs = x_prompt, x_sample
    prompt_states, sample_states = [], []
    for i in range(DEPTH):
        lw = dict(g_mix=g_mix[i], w_in=w_in[i], ssm_a_re=ssm_a_re[i], ssm_a_im=ssm_a_im[i],
                  ssm_log_dt=ssm_log_dt[i], ssm_b_re=ssm_b_re[i], ssm_b_im=ssm_b_im[i],
                  ssm_c_re=ssm_c_re[i], ssm_c_im=ssm_c_im[i], ssm_d=ssm_d[i], ssm_w_glu=ssm_w_glu[i],
                  ssm_b_glu=ssm_b_glu[i], g_q=g_q[i], g_k_cmp=g_k_cmp[i], g_k_sel=g_k_sel[i],
                  g_k_win=g_k_win[i], cmp_pe_k=cmp_pe_k[i], cmp_w1_k=cmp_w1_k[i], cmp_w2_k=cmp_w2_k[i],
                  cmp_pe_v=cmp_pe_v[i], cmp_w1_v=cmp_w1_v[i], cmp_w2_v=cmp_w2_v[i],
                  g_out_ssm=g_out_ssm[i], g_out_attn=g_out_attn[i], w_out=w_out[i], g_ffn=g_ffn[i],
                  peer_w_q=peer_w_q[i], peer_subkeys=peer_subkeys[i], peer_u=peer_u[i], peer_v=peer_v[i],
                  g_ple=g_ple[i], ple_w_gate=ple_w_gate[i], ple_w_proj=ple_w_proj[i])
        u, q, kc, vc, ks, vs, kw, vw, gates = project(hp, pos_p, lw)
        zeros = jnp.zeros((bp, SSM_GROUPS, SSM_STATE), jnp.float32)
        ssm_y, sre, sim = ssm_mixer(u, zeros, zeros, lw)
        attn = attention_prompt(q, kc, vc, ks, vs, kw, vw, gates, lw)
        keep = min(WINDOW, tp)
        prompt_states.append((kc, vc, ks, vs, kw[:, -keep:], vw[:, -keep:], sre, sim))
        hp = finish_layer(hp, ssm_y, attn, p_prompt[i], lw)
        u, q, kc, vc, ks, vs, kw, vw, gates = project(hs, pos_s, lw)
        ssm_y, sre, sim = ssm_mixer(u, state_ssm_re[i], state_ssm_im[i], lw)
        attn, kw_buf, vw_buf = attention_sample(
            q, pos_s, kc, vc, ks, vs, kw, vw, gates,
            gather_pages(cache_k_cmp[i], page_table), gather_pages(cache_v_cmp[i], page_table),
            gather_pages(cache_k_sel[i], page_table), gather_pages(cache_v_sel[i], page_table),
            cache_k_win[i], cache_v_win[i], past_len, lw)
        sample_states.append((kc, vc, ks, vs, kw_buf, vw_buf, sre, sim))
        hs = finish_layer(hs, ssm_y, attn, p_sample[i], lw)
    (p_kc, p_vc, p_ks, p_vs, p_kw, p_vw, p_sre, p_sim) = [jnp.stack(a) for a in zip(*prompt_states)]
    (s_kc, s_vc, s_ks, s_vs, s_kw, s_vw, s_sre, s_sim) = [jnp.stack(a) for a in zip(*sample_states)]
    return (hp, hs, p_kc, p_vc, p_ks, p_vs, p_kw, p_vw, p_sre, p_sim,
            s_kc, s_vc, s_ks, s_vs, s_kw, s_vw, s_sre, s_sim)
```

```python
import functools
import math

import jax
import jax.numpy as jnp
import numpy as np
from jax import lax
from jax.experimental import pallas as pl
from jax.experimental.pallas import tpu as pltpu

F32 = jnp.float32
BF16 = jnp.bfloat16

D_MODEL = 1024
SSM_WIDTH = 512
SSM_CH = 16
SSM_GROUPS = 32
SSM_STATE = 64
ATTN_WIDTH = 512
HEAD_DIM = 64
N_HEADS = 8
N_KV = 2
GQA_REP = 4
KV_WIDTH = 128
ROPE_DIM = 16
ROPE_THETA = 500000.0
CMP_STRIDE = 16
CMP_HIDDEN = 128
SEL_BLOCK = 64
SEL_TOPN = 16
FORCE_BONUS = 1000.0
WINDOW = 512
PEER_HEADS = 8
PEER_KEYS = 128
PEER_TOPK = 16
RMS_EPS = 1e-6
NEG = -1e30
PAD_SCORE = -3.0e38
TAKEN_SCORE = -3.4e38
LANES = 128
VMEM_LIMIT = 56 * 1024 * 1024


def _dot(a, b):
    return jnp.dot(a, b, preferred_element_type=F32)


def _dot_nt(a, b):
    return lax.dot_general(a, b, (((1,), (1,)), ((), ())), preferred_element_type=F32)


def _split_dot(x, w):
    hi = x.astype(BF16)
    lo = (x - hi.astype(F32)).astype(BF16)
    return _dot(hi, w) + _dot(lo, w)


def _gelu(x):
    c = math.sqrt(2.0 / math.pi)
    return 0.5 * x * (1.0 + jnp.tanh(c * (x + 0.044715 * (x * x * x))))


def _sigmoid(x):
    return 1.0 / (1.0 + jnp.exp(-x))


def _tile_lanes(a, reps):
    return a if reps == 1 else jnp.concatenate([a] * reps, axis=1)


def _head_norm_rope(z, ones_bd, gain, c, s1, s2):
    width = z.shape[1]
    ss = _split_dot(z * z, ones_bd)
    zn = z * lax.rsqrt(ss * (1.0 / HEAD_DIM) + RMS_EPS) * gain
    reps = width // LANES
    half = ROPE_DIM // 2
    return (zn * _tile_lanes(c, reps)
            + pltpu.roll(zn, width - half, 1) * _tile_lanes(s1, reps)
            + pltpu.roll(zn, half, 1) * _tile_lanes(s2, reps))


def _cparams(sem):
    return pltpu.CompilerParams(dimension_semantics=sem, vmem_limit_bytes=VMEM_LIMIT)


def _project_kernel(x_ref, gmix_ref, wu_ref, wq_ref, wk_ref, wg_ref, gq_ref, gks_ref, gkw_ref,
                    c_ref, s1_ref, s2_ref, ones512_ref, ones128_ref,
                    u_ref, q_ref, kc_ref, vc_ref, ks_ref, vs_ref, kw_ref, vw_ref,
                    ksb_ref, vsb_ref, kwb_ref, vwb_ref, gate_ref):
    x = x_ref[...]
    a = x * lax.rsqrt(jnp.mean(x * x, axis=-1, keepdims=True) + RMS_EPS) * gmix_ref[...]
    ab = a.astype(BF16)
    c, s1, s2 = c_ref[...], s1_ref[...], s2_ref[...]
    u_ref[...] = _dot(ab, wu_ref[...])
    q = _head_norm_rope(_dot(ab, wq_ref[...]), ones512_ref[...], gq_ref[...], c, s1, s2)
    q_ref[...] = (q * (HEAD_DIM ** -0.5)).astype(BF16)
    zk = _dot(ab, wk_ref[...])
    kc_ref[...] = zk[:, 0:128]
    vc_ref[...] = zk[:, 128:256]
    ks = _head_norm_rope(zk[:, 256:384], ones128_ref[...], gks_ref[...], c, s1, s2)
    ks_ref[...] = ks
    ksb_ref[...] = ks.astype(BF16)
    vs = zk[:, 384:512]
    vs_ref[...] = vs
    vsb_ref[...] = vs.astype(BF16)
    kw = _head_norm_rope(zk[:, 512:640], ones128_ref[...], gkw_ref[...], c, s1, s2)
    kw_ref[...] = kw
    kwb_ref[...] = kw.astype(BF16)
    vw = zk[:, 640:768]
    vw_ref[...] = vw
    vwb_ref[...] = vw.astype(BF16)
    gate_ref[...] = _sigmoid(_dot(ab, wg_ref[...]))


def _project(x2d, rope_tabs, n_rope_tiles, tm, pw):
    n = x2d.shape[0]
    grid = (n // tm,)
    row = lambda w: pl.BlockSpec((tm, w), lambda i: (i, 0))
    full = lambda a: pl.BlockSpec(a.shape, lambda i: (0,) * a.ndim)
    rope_spec = pl.BlockSpec((tm, LANES), lambda i: (i % n_rope_tiles, 0))
    consts = [pw['gmix'], pw['wu'], pw['wq'], pw['wk'], pw['wg'], pw['gq'], pw['gks'], pw['gkw']]
    in_specs = [row(D_MODEL)] + [full(a) for a in consts] + [rope_spec] * 3 + [full(pw['ones512']), full(pw['ones128'])]
    out_shape = ([jax.ShapeDtypeStruct((n, 512), F32), jax.ShapeDtypeStruct((n, 512), BF16)]
                 + [jax.ShapeDtypeStruct((n, 128), F32)] * 6
                 + [jax.ShapeDtypeStruct((n, 128), BF16)] * 4
                 + [jax.ShapeDtypeStruct((n, 128), F32)])
    out_specs = [row(512), row(512)] + [row(128)] * 11
    return pl.pallas_call(
        _project_kernel, grid=grid, in_specs=in_specs, out_specs=out_specs, out_shape=out_shape,
        compiler_params=_cparams(("parallel",)), name="project",
    )(x2d, *consts, *rope_tabs, pw['ones512'], pw['ones128'])


N_PAIR = SSM_GROUPS // 2


def _ssm_kernel(u_ref, s0r_ref, s0i_ref, wb_ref, ar_ref, ai_ref, wcr_ref, wci_ref, d_ref, wglu_ref, bglu_ref,
                y_ref, sr_ref, si_ref, bre, bim, xr_s, xi_s, *, bb, tc):
    ci = pl.program_id(1)

    @pl.when(ci == 0)
    def _():
        xr_s[...] = s0r_ref[...]
        xi_s[...] = s0i_ref[...]

    u = u_ref[...].reshape(bb * tc, SSM_WIDTH)
    ub = u.astype(BF16)
    for j in range(4):
        z = _dot(ub[:, j * 128:(j + 1) * 128], wb_ref[j])
        for pp in range(4):
            pair = 4 * j + pp
            for b in range(bb):
                rows = slice(b * tc, (b + 1) * tc)
                bre[b, pl.ds(pair, tc, stride=N_PAIR), :] = z[rows, pp * 128:(pp + 1) * 128]
                bim[b, pl.ds(pair, tc, stride=N_PAIR), :] = z[rows, 512 + pp * 128:512 + (pp + 1) * 128]

    ar = ar_ref[...]
    ai = ai_ref[...]

    def step(t, carry):
        out = []
        off = pl.multiple_of(t * N_PAIR, N_PAIR)
        for b in range(bb):
            xr, xi = carry[2 * b], carry[2 * b + 1]
            nr = ar * xr - ai * xi + bre[b, pl.ds(off, N_PAIR), :]
            ni = ar * xi + ai * xr + bim[b, pl.ds(off, N_PAIR), :]
            bre[b, pl.ds(off, N_PAIR), :] = nr
            bim[b, pl.ds(off, N_PAIR), :] = ni
            out += [nr, ni]
        return tuple(out)

    init = []
    for b in range(bb):
        init += [xr_s[b], xi_s[b]]
    fin = lax.fori_loop(0, tc, step, tuple(init), unroll=8 if tc >= 8 else tc)
    for b in range(bb):
        xr_s[b] = fin[2 * b]
        xi_s[b] = fin[2 * b + 1]
    sr_ref[...] = xr_s[...]
    si_ref[...] = xi_s[...]

    ys = []
    for b in range(bb):
        tiles = []
        for j in range(4):
            acc = None
            for pp in range(4):
                pair = 4 * j + pp
                xr = bre[b, pl.ds(pair, tc, stride=N_PAIR), :].astype(BF16)
                xi = bim[b, pl.ds(pair, tc, stride=N_PAIR), :].astype(BF16)
                term = _dot(xr, wcr_ref[pair]) - _dot(xi, wci_ref[pair])
                acc = term if acc is None else acc + term
            tiles.append(acc)
        ys.append(jnp.concatenate(tiles, axis=1))
    y = jnp.concatenate(ys, axis=0) if bb > 1 else ys[0]
    y = y + d_ref[...] * u
    zg = _gelu(y)
    out = zg * _sigmoid(_dot(zg.astype(BF16), wglu_ref[...]) + bglu_ref[...])
    y_ref[...] = out.reshape(bb, tc, SSM_WIDTH)


def _ssm(u3, s0r, s0i, pw, bb, tc):
    bsz, t, _ = u3.shape
    grid = (bsz // bb, t // tc)
    full = lambda a: pl.BlockSpec(a.shape, lambda b, c: (0,) * a.ndim)
    st_spec = pl.BlockSpec((bb, N_PAIR, LANES), lambda b, c: (b, 0, 0))
    consts = [pw['ssm_wb'], pw['ssm_ar'], pw['ssm_ai'], pw['ssm_wcr'], pw['ssm_wci'], pw['ssm_d'], pw['ssm_wglu'], pw['ssm_bglu']]
    return pl.pallas_call(
        functools.partial(_ssm_kernel, bb=bb, tc=tc),
        grid=grid,
        in_specs=[pl.BlockSpec((bb, tc, SSM_WIDTH), lambda b, c: (b, c, 0)), st_spec, st_spec] + [full(a) for a in consts],
        out_specs=[pl.BlockSpec((bb, tc, SSM_WIDTH), lambda b, c: (b, c, 0)), st_spec, st_spec],
        out_shape=[jax.ShapeDtypeStruct((bsz, t, SSM_WIDTH), F32),
                   jax.ShapeDtypeStruct((bsz, N_PAIR, LANES), F32),
                   jax.ShapeDtypeStruct((bsz, N_PAIR, LANES), F32)],
        scratch_shapes=[pltpu.VMEM((bb, tc * N_PAIR, LANES), F32), pltpu.VMEM((bb, tc * N_PAIR, LANES), F32),
                        pltpu.VMEM((bb, N_PAIR, LANES), F32), pltpu.VMEM((bb, N_PAIR, LANES), F32)],
        compiler_params=_cparams(("parallel", "arbitrary")), name="ssm",
    )(u3, s0r, s0i, *consts)


def _compress_core(blk_b, w1big, peb, w2bd):
    h = _dot(blk_b, w1big)
    n = h.shape[0]
    pre = h[:, :256] + pltpu.roll(h[:, 256:], n - 1, 0) + peb
    return _dot(_gelu(pre).astype(BF16), w2bd)


def _pe_term(pe_ref, w1_ref):
    t = _dot(pe_ref[...], w1_ref[...])[0:1]
    return jnp.concatenate([t, t], axis=1)


def _compress_kernel(kr_ref, vr_ref, w1k_ref, w1v_ref, pek_ref, pev_ref, w1ko_ref, w1vo_ref, w2k_ref, w2v_ref,
                     gkc_ref, c_ref, s1_ref, s2_ref, ones128_ref, kc_ref, vc_ref):
    kc = _compress_core(kr_ref[0].astype(BF16), w1k_ref[...], _pe_term(pek_ref, w1ko_ref), w2k_ref[...])
    kc = _head_norm_rope(kc, ones128_ref[...], gkc_ref[...], c_ref[...], s1_ref[...], s2_ref[...])
    kc_ref[0] = kc.astype(BF16)
    vc = _compress_core(vr_ref[0].astype(BF16), w1v_ref[...], _pe_term(pev_ref, w1vo_ref), w2v_ref[...])
    vc_ref[0] = vc.astype(BF16)


def _compress_prompt(kc_rows, vc_rows, cw, tabs_c):
    bsz, nblk, _ = kc_rows.shape
    full = lambda a: pl.BlockSpec(a.shape, lambda b: (0,) * a.ndim)
    per_b = lambda w: pl.BlockSpec((1, nblk, w), lambda b: (b, 0, 0))
    consts = [cw['w1big_k'], cw['w1big_v'], cw['pe_k'], cw['pe_v'], cw['w1_k'], cw['w1_v'], cw['w2bd_k'], cw['w2bd_v'],
              cw['gkc'], *tabs_c, cw['ones128']]
    return pl.pallas_call(
        _compress_kernel, grid=(bsz,),
        in_specs=[per_b(2048), per_b(2048)] + [full(a) for a in consts],
        out_specs=[per_b(128), per_b(128)],
        out_shape=[jax.ShapeDtypeStruct((bsz, nblk, 128), BF16)] * 2,
        compiler_params=_cparams(("parallel",)), name="compress_prompt",
    )(kc_rows, vc_rows, *consts)


SEL_TK = 512


def _lane_tile4(a):
    return jnp.concatenate([a, a, a, a], axis=1)


def _top_blocks(score, lane_f, n_take):
    sel = jnp.zeros(score.shape, F32)
    for _ in range(n_take):
        mx = jnp.max(score, axis=1, keepdims=True)
        idx = jnp.min(jnp.where(score == mx, lane_f, 1e9), axis=1, keepdims=True)
        pick = lane_f == idx
        sel = jnp.where(pick, 1.0, sel)
        score = jnp.where(pick, TAKEN_SCORE, score)
    return sel


def _online_update(h, s, v_aug, acc_ref, m_ref):
    m_prev = m_ref[h]
    m_next = jnp.maximum(m_prev, jnp.max(s, axis=1, keepdims=True))
    reps = s.shape[1] // LANES
    p = jnp.exp(s - _tile_lanes(m_next, reps))
    alpha = jnp.exp(m_prev - m_next)
    acc_ref[h] = alpha * acc_ref[h] + _dot(p.astype(BF16), v_aug)
    m_ref[h] = m_next


def _attn_prompt_kernel(q_ref, gate_ref, kc_ref, vc_ref, ks_ref, vs_ref, kw_ref, vw_ref, pool_ref, exp_ref, gexp_ref,
                        o_ref, qh_ref, selb_ref, acc_ref, m_ref, *, tq, t_len):
    ncmp = t_len // CMP_STRIDE
    nsel = t_len // SEL_BLOCK
    qt = pl.program_id(1)
    q0 = qt * tq
    lane = lax.broadcasted_iota(jnp.int32, (tq, LANES), 1)
    lo = lane < HEAD_DIM
    qpos = q0 + lax.broadcasted_iota(jnp.int32, (tq, 1), 0)
    q = q_ref[0]
    zero_b = jnp.zeros((tq, LANES), BF16)
    for h in range(N_HEADS):
        j, half = h % 4, h // 4
        tile = q[:, j * LANES:(j + 1) * LANES]
        qh_ref[h] = jnp.where(lo if half == 0 else jnp.logical_not(lo), tile, zero_b)

    kc = kc_ref[0]
    vc = vc_ref[0]
    cidx = lax.broadcasted_iota(jnp.int32, (tq, ncmp), 1)
    cmask = (cidx * CMP_STRIDE + (2 * CMP_STRIDE - 1) <= qpos) & (cidx < ncmp - 1)
    o_cmp = []
    imp = [jnp.zeros((tq, ncmp), F32), jnp.zeros((tq, ncmp), F32)]
    for h in range(N_HEADS):
        s = jnp.where(cmask, _dot_nt(qh_ref[h], kc), NEG)
        mx = jnp.max(s, axis=1, keepdims=True)
        e = jnp.where(cmask, jnp.exp(s - mx), 0.0)
        l = jnp.sum(e, axis=1, keepdims=True)
        p = e / jnp.where(l > 0.0, l, 1.0)
        o_cmp.append(_dot(p.astype(BF16), vc))
        imp[h // GQA_REP] = imp[h // GQA_REP] + p

    cur = qpos // SEL_BLOCK
    forced = (lane == 0) | (lane == cur) | (lane == cur - 1)
    future = lane > cur
    lane_f = lane.astype(F32)
    for g in range(N_KV):
        sc = _split_dot(imp[g], pool_ref[...])
        sc = jnp.where(future, NEG, jnp.where(forced, sc + FORCE_BONUS, sc))
        sc = jnp.where(lane < nsel, sc, PAD_SCORE)
        selb_ref[g] = _top_blocks(sc, lane_f, min(SEL_TOPN, nsel)).astype(BF16)

    acc_ref[...] = jnp.zeros(acc_ref.shape, F32)
    m_ref[...] = jnp.full(m_ref.shape, NEG, F32)
    ones_b = jnp.ones((1, LANES), BF16)

    def aug_v(v, g, rows):
        lo_k = lax.broadcasted_iota(jnp.int32, (rows, LANES), 1) < HEAD_DIM
        keep = lo_k if g == 0 else jnp.logical_not(lo_k)
        return jnp.where(keep, v, jnp.broadcast_to(ones_b, (rows, LANES)))

    def sel_body(kt, carry):
        k0 = pl.multiple_of(kt * SEL_TK, SEL_TK)
        k = ks_ref[0, pl.ds(k0, SEL_TK), :]
        v = vs_ref[0, pl.ds(k0, SEL_TK), :]
        kpos = k0 + lax.broadcasted_iota(jnp.int32, (1, SEL_TK), 1)
        causal = kpos <= qpos
        for g in range(N_KV):
            mask = (_dot(selb_ref[g], exp_ref[kt]) > 0.5) & causal
            vg = aug_v(v, g, SEL_TK)
            for r in range(GQA_REP):
                h = GQA_REP * g + r
                s = jnp.where(mask, _dot_nt(qh_ref[h], k), NEG)
                _online_update(h, s, vg, acc_ref, m_ref)
        return carry

    n_kt = (q0 + tq + SEL_TK - 1) // SEL_TK
    lax.fori_loop(0, n_kt, sel_body, 0)

    def win_body(w, carry):
        k0 = pl.multiple_of((qt - w) * tq, tq)
        k = kw_ref[0, pl.ds(k0, tq), :]
        v = vw_ref[0, pl.ds(k0, tq), :]
        kpos = k0 + lax.broadcasted_iota(jnp.int32, (1, tq), 1)
        diff = qpos - kpos
        mask = (diff >= 0) & (diff < WINDOW)
        for g in range(N_KV):
            vg = aug_v(v, g, tq)
            for r in range(GQA_REP):
                h = GQA_REP * g + r
                s = jnp.where(mask, _dot_nt(qh_ref[h], k), NEG)
                _online_update(N_HEADS + h, s, vg, acc_ref, m_ref)
        return carry

    lax.fori_loop(0, jnp.minimum(qt, WINDOW // tq) + 1, win_body, 0)

    def normalised(idx):
        a = acc_ref[idx]
        return a / pltpu.roll(a, HEAD_DIM, 1)

    ge = _split_dot(gate_ref[...], gexp_ref[...])
    tiles = []
    for j in range(4):
        oc = jnp.where(lo, o_cmp[j], o_cmp[4 + j])
        osel = jnp.where(lo, normalised(j), normalised(4 + j))
        owin = jnp.where(lo, normalised(N_HEADS + j), normalised(N_HEADS + 4 + j))
        sl = slice(j * LANES, (j + 1) * LANES)
        tiles.append(ge[:, sl] * oc + ge[:, 512 + j * LANES:512 + (j + 1) * LANES] * osel
                     + ge[:, 1024 + j * LANES:1024 + (j + 1) * LANES] * owin)
    o_ref[...] = jnp.concatenate(tiles, axis=1)


def _attn_prompt(q, gate, kc, vc, ksb, vsb, kwb, vwb, aw, bsz, t_len, tq):
    nq = t_len // tq
    ncmp = t_len // CMP_STRIDE
    full = lambda a: pl.BlockSpec(a.shape, lambda b, i: (0,) * a.ndim)
    tok = lambda w: pl.BlockSpec((tq, w), lambda b, i: (b * nq + i, 0))
    per_b_q = pl.BlockSpec((1, tq, 512), lambda b, i: (b, i, 0))
    per_b = lambda n, w: pl.BlockSpec((1, n, w), lambda b, i: (b, 0, 0))
    return pl.pallas_call(
        functools.partial(_attn_prompt_kernel, tq=tq, t_len=t_len),
        grid=(bsz, nq),
        in_specs=[per_b_q, tok(LANES), per_b(ncmp, LANES), per_b(ncmp, LANES)] + [per_b(t_len, LANES)] * 4
                 + [full(aw['pool']), full(aw['expand']), full(aw['gexp'])],
        out_specs=tok(ATTN_WIDTH),
        out_shape=jax.ShapeDtypeStruct((bsz * t_len, ATTN_WIDTH), F32),
        scratch_shapes=[pltpu.VMEM((N_HEADS, tq, LANES), BF16), pltpu.VMEM((N_KV, tq, LANES), BF16),
                        pltpu.VMEM((2 * N_HEADS, tq, LANES), F32), pltpu.VMEM((2 * N_HEADS, tq, LANES), F32)],
        compiler_params=_cparams(("parallel", "arbitrary")), name="attn_prompt",
    )(q.reshape(bsz, t_len, 512), gate, kc, vc, ksb.reshape(bsz, t_len, LANES), vsb.reshape(bsz, t_len, LANES),
      kwb.reshape(bsz, t_len, LANES), vwb.reshape(bsz, t_len, LANES), aw['pool'], aw['expand'], aw['gexp'])


def _attn_consts(t_len):
    ncmp = t_len // CMP_STRIDE
    c = np.arange(ncmp)
    pool = (c[:, None] // (SEL_BLOCK // CMP_STRIDE) == np.arange(LANES)[None, :])
    n_kt = max(t_len // SEL_TK, 1)
    key = np.arange(n_kt * SEL_TK).reshape(n_kt, 1, SEL_TK)
    expand = (key // SEL_BLOCK == np.arange(LANES)[None, :, None])
    perm_head = _q_perm() // HEAD_DIM
    rows = np.arange(LANES)[:, None]
    gexp = np.concatenate([rows == (br * N_HEADS + perm_head)[None, :] for br in range(3)], axis=1)
    return {'pool': jnp.asarray(pool, BF16), 'expand': jnp.asarray(expand, BF16), 'gexp': jnp.asarray(gexp, BF16)}


def _compress_weights(pe, w1, w2, tag, cw):
    w1r = w1.reshape(2, CMP_STRIDE, HEAD_DIM, CMP_HIDDEN)
    eye = jnp.eye(N_KV, dtype=F32)
    big = jnp.einsum('srdh,ge->rgdseh', w1r, eye).reshape(CMP_STRIDE * KV_WIDTH, 2 * N_KV * CMP_HIDDEN)
    cw['w1big_' + tag] = big.astype(BF16)
    cw['w1_' + tag] = w1.astype(BF16)
    cw['pe_' + tag] = jnp.tile(pe.reshape(1, -1), (8, 1)).astype(BF16)
    cw['w2bd_' + tag] = jnp.einsum('hd,ge->ghed', w2, eye).reshape(N_KV * CMP_HIDDEN, KV_WIDTH).astype(BF16)


PAGE = 128
CHUNK_PAGES = 32
CHUNK_ROWS = CHUNK_PAGES * PAGE
DEC_Q = 8


def _start_pages(pt_ref, b, c, pools, bufs, sem, slot, dst_of):
    def body(p, carry):
        page = pt_ref[b, c * CHUNK_PAGES + p]
        for i, (pool, buf) in enumerate(zip(pools, bufs)):
            pltpu.make_async_copy(pool.at[page], dst_of(buf, slot, p), sem.at[i, slot]).start()
        return carry
    lax.fori_loop(0, CHUNK_PAGES, body, 0)


def _wait_pages(pools, bufs, sem, slot, dst_of):
    def body(p, carry):
        for i, (pool, buf) in enumerate(zip(pools, bufs)):
            pltpu.make_async_copy(pool.at[0], dst_of(buf, slot, p), sem.at[i, slot]).wait()
        return carry
    lax.fori_loop(0, CHUNK_PAGES, body, 0)


def _gather_step(pt_ref, pools, bufs, sem, dst_of, n_chunks):
    b, c = pl.program_id(0), pl.program_id(1)
    step = b * n_chunks + c
    slot = step % 2
    total = pl.num_programs(0) * n_chunks

    @pl.when(step == 0)
    def _():
        _start_pages(pt_ref, 0, 0, pools, bufs, sem, 0, dst_of)

    @pl.when(step + 1 < total)
    def _():
        nxt = step + 1
        _start_pages(pt_ref, nxt // n_chunks, nxt % n_chunks, pools, bufs, sem, 1 - slot, dst_of)

    _wait_pages(pools, bufs, sem, slot, dst_of)
    return slot


def _rows_dst(buf, slot, p):
    return buf.at[slot, pl.ds(p * PAGE, PAGE), :]


def _lanes_dst(buf, slot, p):
    return buf.at[slot, :, :, pl.ds(p * PAGE, PAGE)]


def _stack_heads_masked(q):
    lo = lax.broadcasted_iota(jnp.int32, (DEC_Q, LANES), 1) < HEAD_DIM
    zero = jnp.zeros((DEC_Q, LANES), q.dtype)
    rows = []
    for h in range(N_HEADS):
        j, half = h % 4, h // 4
        tile = q[:, j * LANES:(j + 1) * LANES]
        rows.append(jnp.where(lo if half == 0 else jnp.logical_not(lo), tile, zero))
    return jnp.concatenate(rows, axis=0)


def _sample_cmp_kernel(pt_ref, kpool, vpool, knew_ref, vnew_ref, q_ref, w1k_ref, w1v_ref, pek_ref, pev_ref,
                       w1ko_ref, w1vo_ref, w2k_ref, w2v_ref, gkc_ref, c_ref, s1_ref, s2_ref, ones128_ref,
                       pool_ref, blkid_ref, ocmp_ref, sel_ref, kbuf, vbuf, sem, hk, hv, *, n_chunks, past_len):
    c = pl.program_id(1)
    slot = _gather_step(pt_ref, (kpool, vpool), (kbuf, vbuf), sem, _rows_dst, n_chunks)
    nhb_chunk = CHUNK_ROWS // CMP_STRIDE
    for buf, w1_ref, h_ref in ((kbuf, w1k_ref, hk), (vbuf, w1v_ref, hv)):
        acc = None
        for r in range(CMP_STRIDE):
            rows = buf[slot, pl.ds(r, nhb_chunk, stride=CMP_STRIDE), :].astype(BF16)
            t = _dot(rows, w1_ref[r])
            acc = t if acc is None else acc + t
        h_ref[pl.ds(pl.multiple_of(c * nhb_chunk, nhb_chunk), nhb_chunk), :] = acc

    @pl.when(c == n_chunks - 1)
    def _():
        nhb = n_chunks * nhb_chunk
        ncp = nhb + 8
        n_valid = nhb + 3
        for new_ref, w1_ref, h_ref in ((knew_ref, w1k_ref, hk), (vnew_ref, w1v_ref, hv)):
            new_b = new_ref[0].astype(BF16)
            hn = None
            for r in range(DEC_Q):
                t = _dot(new_b, w1_ref[r])[r:r + 1]
                hn = t if hn is None else hn + t
            row = lax.broadcasted_iota(jnp.int32, (8, 4 * CMP_HIDDEN), 0)
            h_ref[pl.ds(nhb, 8), :] = jnp.where(row == 0, jnp.broadcast_to(hn, (8, 4 * CMP_HIDDEN)), 0.0)

        def mlp(h_ref, pe_ref, w1o_ref, w2_ref):
            h = h_ref[...]
            pre = h[:, :256] + pltpu.roll(h[:, 256:], ncp - 1, 0) + _pe_term(pe_ref, w1o_ref)
            return _dot(_gelu(pre).astype(BF16), w2_ref[...])

        kc = mlp(hk, pek_ref, w1ko_ref, w2k_ref)
        kc = _head_norm_rope(kc, ones128_ref[...], gkc_ref[...], c_ref[...], s1_ref[...], s2_ref[...]).astype(BF16)
        vc = mlp(hv, pev_ref, w1vo_ref, w2v_ref).astype(BF16)

        qm = _stack_heads_masked(q_ref[0])
        rowq = lax.broadcasted_iota(jnp.int32, (N_HEADS * DEC_Q, 1), 0) % DEC_Q
        qpos = past_len + rowq
        cidx = lax.broadcasted_iota(jnp.int32, (N_HEADS * DEC_Q, ncp), 1)
        cmask = (cidx * CMP_STRIDE + (2 * CMP_STRIDE - 1) <= qpos) & (cidx < n_valid)
        s = jnp.where(cmask, _dot_nt(qm, kc), NEG)
        mx = jnp.max(s, axis=1, keepdims=True)
        e = jnp.where(cmask, jnp.exp(s - mx), 0.0)
        l = jnp.sum(e, axis=1, keepdims=True)
        p = e / jnp.where(l > 0.0, l, 1.0)
        ocmp_ref[0] = _dot(p.astype(BF16), vc)

        imps = []
        for g in range(N_KV):
            base = g * GQA_REP * DEC_Q
            acc = p[base:base + DEC_Q]
            for r in range(1, GQA_REP):
                acc = acc + p[base + r * DEC_Q:base + (r + 1) * DEC_Q]
            imps.append(acc)
        imp = jnp.concatenate(imps, axis=0)
        sc = _split_dot(imp, pool_ref[...])
        nsl = sc.shape[1]
        blk = jnp.broadcast_to(blkid_ref[...], (N_KV * DEC_Q, nsl))
        cur = past_len // SEL_BLOCK
        forced = (blk == 0) | (blk == cur) | (blk == cur - 1)
        future = blk > cur
        sc = jnp.where(future, NEG, jnp.where(forced, sc + FORCE_BONUS, sc))
        sc = jnp.where(blk >= 0, sc, PAD_SCORE)
        lane_f = lax.broadcasted_iota(jnp.int32, (N_KV * DEC_Q, nsl), 1).astype(F32)
        sel = _top_blocks(sc, lane_f, SEL_TOPN).astype(BF16)
        for t in range(nsl // LANES):
            sel_ref[0, t] = sel[:, t * LANES:(t + 1) * LANES]


def _sample_cmp(page_table, kpool, vpool, knew, vnew, q, cw, tabs_c, sw, past_len):
    bsz, n_pages = page_table.shape
    n_chunks = n_pages // CHUNK_PAGES
    ncp = n_pages * (PAGE // CMP_STRIDE) + 8
    nsl = (n_chunks + 1) * LANES
    full = lambda a: pl.BlockSpec(a.shape, lambda b, c, pt: (0,) * a.ndim)
    per_b = lambda s1, s2: pl.BlockSpec((1, s1, s2), lambda b, c, pt: (b, 0, 0))
    any_spec = pl.BlockSpec(memory_space=pl.ANY)
    w1k = cw['w1big_k'].reshape(CMP_STRIDE, KV_WIDTH, 4 * CMP_HIDDEN)
    w1v = cw['w1big_v'].reshape(CMP_STRIDE, KV_WIDTH, 4 * CMP_HIDDEN)
    consts = [w1k, w1v, cw['pe_k'], cw['pe_v'], cw['w1_k'], cw['w1_v'], cw['w2bd_k'], cw['w2bd_v'],
              cw['gkc'], *tabs_c, cw['ones128'], sw['pool'], sw['blkid']]
    grid_spec = pltpu.PrefetchScalarGridSpec(
        num_scalar_prefetch=1, grid=(bsz, n_chunks),
        in_specs=[any_spec, any_spec, per_b(DEC_Q, LANES), per_b(DEC_Q, LANES), per_b(DEC_Q, 512)] + [full(a) for a in consts],
        out_specs=[per_b(N_HEADS * DEC_Q, LANES),
                   pl.BlockSpec((1, n_chunks + 1, N_KV * DEC_Q, LANES), lambda b, c, pt: (b, 0, 0, 0))],
        scratch_shapes=[pltpu.VMEM((2, CHUNK_ROWS, LANES), F32), pltpu.VMEM((2, CHUNK_ROWS, LANES), F32),
                        pltpu.SemaphoreType.DMA((2, 2)),
                        pltpu.VMEM((ncp, 4 * CMP_HIDDEN), F32), pltpu.VMEM((ncp, 4 * CMP_HIDDEN), F32)])
    return pl.pallas_call(
        functools.partial(_sample_cmp_kernel, n_chunks=n_chunks, past_len=past_len),
        grid_spec=grid_spec,
        out_shape=[jax.ShapeDtypeStruct((bsz, N_HEADS * DEC_Q, LANES), F32),
                   jax.ShapeDtypeStruct((bsz, n_chunks + 1, N_KV * DEC_Q, LANES), BF16)],
        compiler_params=_cparams(("arbitrary", "arbitrary")), name="sample_cmp",
    )(page_table, kpool, vpool, knew, vnew, q, *consts)


def _sample_consts(n_pages):
    n_chunks = n_pages // CHUNK_PAGES
    nsl = (n_chunks + 1) * LANES
    ns = n_pages * (PAGE // SEL_BLOCK) + 1
    lane = np.arange(nsl)
    blk = (lane // LANES) * (CHUNK_ROWS // SEL_BLOCK) + lane % LANES
    blkid = np.where((lane % LANES < CHUNK_ROWS // SEL_BLOCK) & (blk < ns), blk, -1).astype(np.int32)
    ncp = n_pages * (PAGE // CMP_STRIDE) + 8
    pool = (np.arange(ncp)[:, None] // (SEL_BLOCK // CMP_STRIDE)) == blkid[None, :]
    key = np.arange(CHUNK_ROWS)
    expand = (key[None, :] // SEL_BLOCK == np.arange(LANES)[:, None])
    return {'pool': jnp.asarray(pool, BF16), 'blkid': jnp.asarray(blkid.reshape(1, nsl)),
            'expand': jnp.asarray(expand, BF16)}


def _flash_rows(s, mask, vt, m_ref, l_ref, acc_ref, idx):
    s = jnp.where(mask, s, NEG)
    m_prev = m_ref[idx]
    m_next = jnp.maximum(m_prev, jnp.max(s, axis=1, keepdims=True))
    p = jnp.exp(s - m_next[:, 0:1])
    alpha = jnp.exp(m_prev - m_next)
    l_ref[idx] = alpha * l_ref[idx] + jnp.sum(p, axis=1, keepdims=True)
    acc_ref[idx] = alpha[:, :HEAD_DIM] * acc_ref[idx] + _dot_nt(p.astype(BF16), vt)
    m_ref[idx] = m_next


def _sample_sel_kernel(pt_ref, kpool, vpool, q_ref, ksn_ref, vsn_ref, kwt_ref, vwt_ref, kwn_ref, vwn_ref,
                       sel_ref, ocmp_ref, gate_ref, exp_ref, gexp_ref, o_ref,
                       kbuf, vbuf, sem, m_ref, l_ref, acc_ref, *, n_chunks, past_len):
    c = pl.program_id(1)
    slot = _gather_step(pt_ref, (kpool, vpool), (kbuf, vbuf), sem, _lanes_dst, n_chunks)
    rows = GQA_REP * DEC_Q
    q = q_ref[0]

    def group_q(g):
        return jnp.concatenate([q[:, r * LANES + g * HEAD_DIM:r * LANES + (g + 1) * HEAD_DIM] for r in range(GQA_REP)], axis=0)

    @pl.when(c == 0)
    def _():
        m_ref[...] = jnp.full(m_ref.shape, NEG, F32)
        l_ref[...] = jnp.zeros(l_ref.shape, F32)
        acc_ref[...] = jnp.zeros(acc_ref.shape, F32)

    def four(a):
        return jnp.concatenate([a] * GQA_REP, axis=0)

    mexp = _dot(sel_ref[0, c], exp_ref[...]) > 0.5
    for g in range(N_KV):
        kt = kbuf[slot, g].astype(BF16)
        vt = vbuf[slot, g].astype(BF16)
        s = _dot(group_q(g), kt)
        _flash_rows(s, four(mexp[g * DEC_Q:(g + 1) * DEC_Q]), vt, m_ref, l_ref, acc_ref, g)

    @pl.when(c == n_chunks - 1)
    def _():
        qm = _stack_heads_masked(q)
        rowq = lax.broadcasted_iota(jnp.int32, (N_HEADS * DEC_Q, 1), 0) % DEC_Q
        col = lax.broadcasted_iota(jnp.int32, (N_HEADS * DEC_Q, LANES), 1)
        new_mask = col <= rowq
        lo = col < HEAD_DIM

        def new_keys_step(kn_ref, vn_ref, base, extra_mask):
            s = _dot_nt(qm, kn_ref[0])
            msk = new_mask & extra_mask
            for g in range(N_KV):
                sl = slice(g * rows, (g + 1) * rows)
                vt_full = vn_ref[0]
                vt = vt_full[:, g * HEAD_DIM:(g + 1) * HEAD_DIM]
                sg = jnp.where(msk[sl], s[sl], NEG)
                idx = base + g
                m_prev = m_ref[idx]
                m_next = jnp.maximum(m_prev, jnp.max(sg, axis=1, keepdims=True))
                p = jnp.exp(sg - m_next[:, 0:1])
                alpha = jnp.exp(m_prev - m_next)
                l_ref[idx] = alpha * l_ref[idx] + jnp.sum(p, axis=1, keepdims=True)
                acc_ref[idx] = alpha[:, :HEAD_DIM] * acc_ref[idx] + _dot(p.astype(BF16), vt)
                m_ref[idx] = m_next

        sel_new = sel_ref[0, n_chunks][:, 0:1].astype(F32) > 0.5
        sel_rows = jnp.concatenate([four(sel_new[g * DEC_Q:(g + 1) * DEC_Q]) for g in range(N_KV)], axis=0)
        new_keys_step(ksn_ref, vsn_ref, 0, sel_rows)

        wcol = lax.broadcasted_iota(jnp.int32, (rows, WINDOW), 1)
        wq = lax.broadcasted_iota(jnp.int32, (rows, 1), 0) % DEC_Q
        wmask = wcol > wq
        for g in range(N_KV):
            kt = kwt_ref[0, g].astype(BF16)
            vt = vwt_ref[0, g].astype(BF16)
            _flash_rows(_dot(group_q(g), kt), wmask, vt, m_ref, l_ref, acc_ref, N_KV + g)
        new_keys_step(kwn_ref, vwn_ref, N_KV, jnp.full((N_HEADS * DEC_Q, 1), True))

        ge = _split_dot(gate_ref[0], gexp_ref[...])
        oc = ocmp_ref[0]
        lo8 = lo[:DEC_Q]
        tiles = []
        for r in range(GQA_REP):
            rs = slice(r * DEC_Q, (r + 1) * DEC_Q)
            osel = jnp.concatenate([acc_ref[g][rs] / l_ref[g][rs, 0:1] for g in range(N_KV)], axis=1)
            owin = jnp.concatenate([acc_ref[N_KV + g][rs] / l_ref[N_KV + g][rs, 0:1] for g in range(N_KV)], axis=1)
            ocr = jnp.where(lo8, oc[rs], oc[GQA_REP * DEC_Q + r * DEC_Q:GQA_REP * DEC_Q + (r + 1) * DEC_Q])
            tiles.append(ge[:, r * LANES:(r + 1) * LANES] * ocr + ge[:, 512 + r * LANES:512 + (r + 1) * LANES] * osel
                         + ge[:, 1024 + r * LANES:1024 + (r + 1) * LANES] * owin)
        o_ref[0] = jnp.concatenate(tiles, axis=1)


def _sample_sel(page_table, kpool_t, vpool_t, q, ksn, vsn, kwt, vwt, kwn, vwn, sel, ocmp, gate, sw, gexp, past_len):
    bsz, n_pages = page_table.shape
    n_chunks = n_pages // CHUNK_PAGES
    full = lambda a: pl.BlockSpec(a.shape, lambda b, c, pt: (0,) * a.ndim)
    per_b = lambda *s: pl.BlockSpec((1,) + tuple(s), lambda b, c, pt: (b,) + (0,) * len(s))
    any_spec = pl.BlockSpec(memory_space=pl.ANY)
    rows = GQA_REP * DEC_Q
    grid_spec = pltpu.PrefetchScalarGridSpec(
        num_scalar_prefetch=1, grid=(bsz, n_chunks),
        in_specs=[any_spec, any_spec, per_b(DEC_Q, 512), per_b(LANES, LANES), per_b(LANES, LANES),
                  per_b(N_KV, HEAD_DIM, WINDOW), per_b(N_KV, HEAD_DIM, WINDOW), per_b(LANES, LANES), per_b(LANES, LANES),
                  per_b(n_chunks + 1, N_KV * DEC_Q, LANES), per_b(N_HEADS * DEC_Q, LANES), per_b(DEC_Q, LANES),
                  full(sw['expand']), full(gexp)],
        out_specs=per_b(DEC_Q, ATTN_WIDTH),
        scratch_shapes=[pltpu.VMEM((2, N_KV, HEAD_DIM, CHUNK_ROWS), F32), pltpu.VMEM((2, N_KV, HEAD_DIM, CHUNK_ROWS), F32),
                        pltpu.SemaphoreType.DMA((2, 2)),
                        pltpu.VMEM((2 * N_KV, rows, LANES), F32), pltpu.VMEM((2 * N_KV, rows, LANES), F32),
                        pltpu.VMEM((2 * N_KV, rows, HEAD_DIM), F32)])
    return pl.pallas_call(
        functools.partial(_sample_sel_kernel, n_chunks=n_chunks, past_len=past_len),
        grid_spec=grid_spec,
        out_shape=jax.ShapeDtypeStruct((bsz, DEC_Q, ATTN_WIDTH), F32),
        compiler_params=_cparams(("arbitrary", "arbitrary")), name="sample_sel",
    )(page_table, kpool_t, vpool_t, q, ksn, vsn, kwt, vwt, kwn, vwn, sel, ocmp, gate, sw['expand'], gexp)


def _rms(x, g):
    return x * lax.rsqrt(jnp.mean(x * x, axis=-1, keepdims=True) + RMS_EPS) * g


def _outproj_kernel(x_ref, y_ref, a_ref, gos_ref, goa_ref, wa_ref, wb_ref, gffn_ref, h_ref, xn_ref):
    mixed = (_dot(_rms(y_ref[...], gos_ref[...]).astype(BF16), wa_ref[...])
             + _dot(_rms(a_ref[...], goa_ref[...]).astype(BF16), wb_ref[...]))
    h = x_ref[...] + mixed
    h_ref[...] = h
    xn_ref[...] = _rms(h, gffn_ref[...]).astype(BF16)


def _outproj(x2d, y, a, fw, tm):
    n = x2d.shape[0]
    row = lambda w: pl.BlockSpec((tm, w), lambda i: (i, 0))
    full = lambda arr: pl.BlockSpec(arr.shape, lambda i: (0,) * arr.ndim)
    consts = [fw['gos'], fw['goa'], fw['wout_a'], fw['wout_b'], fw['gffn']]
    return pl.pallas_call(
        _outproj_kernel, grid=(n // tm,),
        in_specs=[row(D_MODEL), row(512), row(512)] + [full(c) for c in consts],
        out_specs=[row(D_MODEL), row(D_MODEL)],
        out_shape=[jax.ShapeDtypeStruct((n, D_MODEL), F32), jax.ShapeDtypeStruct((n, D_MODEL), BF16)],
        compiler_params=_cparams(("parallel",)), name="outproj",
    )(x2d, y, a, *consts)


def _cex(a, b):
    return jnp.maximum(a, b), jnp.minimum(a, b)


def _bitonic_merge_desc(x):
    n = len(x)
    j = n // 2
    while j >= 1:
        for i in range(n):
            l = i ^ j
            if l > i:
                x[i], x[l] = _cex(x[i], x[l])
        j //= 2
    return x


def _bitonic_sort_desc(x):
    n = len(x)
    k = 2
    while k <= n:
        j = k // 2
        while j >= 1:
            for i in range(n):
                l = i ^ j
                if l > i:
                    hi, lo = _cex(x[i], x[l])
                    x[i], x[l] = (hi, lo) if (i & k) == 0 else (lo, hi)
            j //= 2
        k *= 2
    return x


def _merge_sublanes_top(x):
    n = len(x)
    for shift in (4, 2, 1):
        other = [pltpu.roll(a, shift, 0) for a in x]
        x = _bitonic_merge_desc([jnp.maximum(x[i], other[n - 1 - i]) for i in range(n)])
    return x


def _peer_select_kernel(xn_ref, wqt_ref, sub_ref, s_ref, st_ref, *, tm):
    qt = _dot_nt(wqt_ref[...], xn_ref[...]).astype(BF16)
    sub8 = lax.broadcasted_iota(jnp.int32, (8, tm), 0)
    tops = []
    for hs in range(2 * PEER_HEADS):
        s = _dot(sub_ref[hs], qt[hs * PEER_KEYS:(hs + 1) * PEER_KEYS])
        s_ref[hs] = s
        x = _bitonic_sort_desc([s[8 * v:8 * v + 8] for v in range(PEER_TOPK)])
        tops.append(_merge_sublanes_top(x))
    for h in range(PEER_HEADS):
        v0, v1 = tops[2 * h], tops[2 * h + 1]
        lo = jnp.zeros((8, tm), F32)
        hi = jnp.zeros((8, tm), F32)
        for b in range(8):
            lo = jnp.where(sub8 == b, v1[b], lo)
            hi = jnp.where(sub8 == b, v1[8 + b], hi)
        l_lo = [v0[a] + lo for a in range(PEER_TOPK)]
        l_hi = [v0[a] + hi for a in range(PEER_TOPK)]
        c = _bitonic_merge_desc([jnp.maximum(l_lo[i], l_hi[PEER_TOPK - 1 - i]) for i in range(PEER_TOPK)])
        c = _merge_sublanes_top(c)
        z = jnp.zeros((8, tm), F32)
        for r in range(PEER_TOPK):
            z = z + jnp.exp(c[r] - c[0])
        row = lax.broadcasted_iota(jnp.int32, (8, tm), 0)
        st = jnp.where(row == 0, c[PEER_TOPK - 1],
                       jnp.where(row == 1, v0[0], jnp.where(row == 2, v1[0], jnp.where(row == 3, 1.0 / z, 0.0))))
        st_ref[h] = st


def _peer_select(xn, wqt, sub, tm):
    n = xn.shape[0]
    return pl.pallas_call(
        functools.partial(_peer_select_kernel, tm=tm), grid=(n // tm,),
        in_specs=[pl.BlockSpec((tm, D_MODEL), lambda i: (i, 0)),
                  pl.BlockSpec(wqt.shape, lambda i: (0, 0)), pl.BlockSpec(sub.shape, lambda i: (0, 0, 0))],
        out_specs=[pl.BlockSpec((2 * PEER_HEADS, PEER_KEYS, tm), lambda i: (0, 0, i)),
                   pl.BlockSpec((PEER_HEADS, 8, tm), lambda i: (0, 0, i))],
        out_shape=[jax.ShapeDtypeStruct((2 * PEER_HEADS, PEER_KEYS, n), F32),
                   jax.ShapeDtypeStruct((PEER_HEADS, 8, n), F32)],
        compiler_params=_cparams(("parallel",)), name="peer_select",
    )(xn, wqt, sub)


def _peer_dense_kernel(xn_ref, u_ref, vt_ref, s_ref, st_ref, h_ref, o_ref, a0_ref, p1_ref, coef_ref, acc_ref, *, tn, te):
    j = pl.program_id(1)

    @pl.when(j == 0)
    def _():
        acc_ref[...] = jnp.zeros(acc_ref.shape, F32)
        for h in range(PEER_HEADS):
            st = st_ref[h]
            a0_ref[h] = jnp.exp(s_ref[2 * h] - st[1:2]) * st[3:4]
            p1_ref[h] = jnp.exp(s_ref[2 * h + 1] - st[2:3])

    act = _dot_nt(u_ref[...], xn_ref[...])
    n_i = te // PEER_KEYS
    i0 = pl.multiple_of(j * n_i, n_i)
    for ii in range(n_i):
        for lt in range(tn // LANES):
            ls = slice(lt * LANES, (lt + 1) * LANES)
            w = jnp.zeros((PEER_KEYS, LANES), F32)
            for h in range(PEER_HEADS):
                s0 = s_ref[2 * h, pl.ds(i0, n_i), ls][ii:ii + 1]
                a0 = a0_ref[h, pl.ds(i0, n_i), ls][ii:ii + 1]
                thr = st_ref[h, 0:1, ls]
                hit = (s_ref[2 * h + 1, :, ls] + s0) >= thr
                w = w + jnp.where(hit, p1_ref[h, :, ls], 0.0) * a0
            coef_ref[ii * PEER_KEYS:(ii + 1) * PEER_KEYS, ls] = (w * _gelu(act[ii * PEER_KEYS:(ii + 1) * PEER_KEYS, ls])).astype(BF16)
    acc_ref[...] += _dot(vt_ref[...], coef_ref[...])

    @pl.when(j == pl.num_programs(1) - 1)
    def _():
        o_ref[...] = h_ref[...] + acc_ref[...].T


def _peer_dense(xn, u_b, vt_b, s_all, stats, h1, tn, te):
    n = xn.shape[0]
    n_exp = u_b.shape[0]
    return pl.pallas_call(
        functools.partial(_peer_dense_kernel, tn=tn, te=te), grid=(n // tn, n_exp // te),
        in_specs=[pl.BlockSpec((tn, D_MODEL), lambda i, j: (i, 0)),
                  pl.BlockSpec((te, D_MODEL), lambda i, j: (j, 0)),
                  pl.BlockSpec((D_MODEL, te), lambda i, j: (0, j)),
                  pl.BlockSpec((2 * PEER_HEADS, PEER_KEYS, tn), lambda i, j: (0, 0, i)),
                  pl.BlockSpec((PEER_HEADS, 8, tn), lambda i, j: (0, 0, i)),
                  pl.BlockSpec((tn, D_MODEL), lambda i, j: (i, 0))],
        out_specs=pl.BlockSpec((tn, D_MODEL), lambda i, j: (i, 0)),
        out_shape=jax.ShapeDtypeStruct((n, D_MODEL), F32),
        scratch_shapes=[pltpu.VMEM((PEER_HEADS, PEER_KEYS, tn), F32), pltpu.VMEM((PEER_HEADS, PEER_KEYS, tn), F32),
                        pltpu.VMEM((te, tn), BF16), pltpu.VMEM((D_MODEL, tn), F32)],
        compiler_params=_cparams(("parallel", "arbitrary")), name="peer_dense",
    )(xn, u_b, vt_b, s_all, stats, h1)


def _ple_kernel(h_ref, p_ref, gple_ref, wg_ref, wp_ref, o_ref):
    h = h_ref[...]
    gate = _sigmoid(_dot(_rms(h, gple_ref[...]).astype(BF16), wg_ref[...]))
    o_ref[...] = h + gate * _dot(p_ref[...].astype(BF16), wp_ref[...])


def _ple(h2, p2d, fw, tm):
    n = h2.shape[0]
    row = lambda w: pl.BlockSpec((tm, w), lambda i: (i, 0))
    full = lambda arr: pl.BlockSpec(arr.shape, lambda i: (0,) * arr.ndim)
    consts = [fw['gple'], fw['wgate'], fw['wproj']]
    return pl.pallas_call(
        _ple_kernel, grid=(n // tm,),
        in_specs=[row(D_MODEL), row(p2d.shape[1])] + [full(c) for c in consts],
        out_specs=row(D_MODEL), out_shape=jax.ShapeDtypeStruct((n, D_MODEL), F32),
        compiler_params=_cparams(("parallel",)), name="ple",
    )(h2, p2d, *consts)


def _finish(x2d, ssm_y, attn_o, ple2d, fw, tm, tn, te):
    h1, xn = _outproj(x2d, ssm_y, attn_o, fw, tm)
    s_all, stats = _peer_select(xn, fw['wqt'], fw['subkeys'], 256)
    h2 = _peer_dense(xn, fw['u_b'], fw['vt_b'], s_all, stats, h1, tn, te)
    return _ple(h2, ple2d, fw, tm)


def _finish_weights(g_out_ssm, g_out_attn, w_out, g_ffn, peer_w_q, peer_subkeys, peer_u, peer_v, g_ple, ple_w_gate, ple_w_proj):
    perm = _q_perm()
    fw = {}
    fw['gos'] = g_out_ssm.reshape(1, SSM_WIDTH)
    fw['goa'] = g_out_attn[perm].reshape(1, ATTN_WIDTH)
    fw['wout_a'] = w_out[:SSM_WIDTH].astype(BF16)
    fw['wout_b'] = w_out[SSM_WIDTH:][perm].astype(BF16)
    fw['gffn'] = g_ffn.reshape(1, D_MODEL)
    fw['wqt'] = peer_w_q.T.astype(BF16)
    fw['subkeys'] = peer_subkeys.reshape(2 * PEER_HEADS, PEER_KEYS, -1).astype(BF16)
    fw['u_b'] = peer_u.astype(BF16)
    fw['vt_b'] = peer_v.T.astype(BF16)
    fw['gple'] = g_ple.reshape(1, D_MODEL)
    fw['wgate'] = ple_w_gate.astype(BF16)
    fw['wproj'] = ple_w_proj.astype(BF16)
    return fw


def _sample_attention(proj, page_table, pool_kc, pool_vc, pool_ks, pool_vs, win_k, win_v, cw, tabs_c, sw, gexp,
                      past_len, bsz):
    q, kc_new, vc_new, gate = proj[1], proj[2], proj[3], proj[12]
    ksb, vsb, kwb, vwb = proj[8], proj[9], proj[10], proj[11]
    n_pool = pool_kc.shape[0]
    per_b = lambda a, w: a.reshape(bsz, DEC_Q, w)
    pad_new = lambda a: jnp.pad(per_b(a, LANES), ((0, 0), (0, LANES - DEC_Q), (0, 0)))
    rows_view = lambda p: p.reshape(n_pool, PAGE, KV_WIDTH)
    kt_view = lambda p: jnp.transpose(p, (0, 2, 3, 1))
    ocmp, sel = _sample_cmp(page_table, rows_view(pool_kc), rows_view(pool_vc), per_b(kc_new, LANES), per_b(vc_new, LANES),
                            per_b(q, 512), cw, tabs_c, sw, past_len)
    return _sample_sel(page_table, kt_view(pool_ks), kt_view(pool_vs), per_b(q, 512), pad_new(ksb), pad_new(vsb),
                       kt_view(win_k), kt_view(win_v), pad_new(kwb), pad_new(vwb), sel, ocmp, per_b(gate, LANES),
                       sw, gexp, past_len)


def _q_perm():
    idx = []
    for j in range(4):
        for half in range(2):
            h = j + 4 * half
            idx += [h * HEAD_DIM + d for d in range(HEAD_DIM)]
    return np.asarray(idx, np.int32)


def _ones_bd(width):
    i = np.arange(width)
    return jnp.asarray((i[:, None] // HEAD_DIM) == (i[None, :] // HEAD_DIM), BF16)


def _rope_tables(pos):
    half = ROPE_DIM // 2
    inv = jnp.power(ROPE_THETA, -jnp.arange(half, dtype=F32) / half)
    ang = pos.astype(F32)[:, None] * inv[None, :]
    cos, sin = jnp.cos(ang), jnp.sin(ang)
    n = pos.shape[0]
    ones = jnp.ones((n, HEAD_DIM - ROPE_DIM), F32)
    zeros = jnp.zeros((n, HEAD_DIM - ROPE_DIM), F32)
    z8 = jnp.zeros((n, half), F32)
    c = jnp.concatenate([cos, cos, ones], axis=1)
    s1 = jnp.concatenate([-sin, z8, zeros], axis=1)
    s2 = jnp.concatenate([z8, sin, zeros], axis=1)
    two = lambda a: jnp.concatenate([a, a], axis=1)
    return two(c), two(s1), two(s2)


def _prep_weights(g_mix, w_in, ssm_a_re, ssm_a_im, ssm_log_dt, ssm_b_re, ssm_b_im, ssm_c_re, ssm_c_im, ssm_d,
                  ssm_w_glu, ssm_b_glu, g_q, g_k_sel, g_k_win):
    pw = {}
    perm = _q_perm()
    w = w_in
    pw['gmix'] = g_mix.reshape(1, D_MODEL)
    pw['wu'] = w[:, 0:512].astype(BF16)
    pw['wq'] = w[:, 512:1024][:, perm].astype(BF16)
    pw['wk'] = w[:, 1024:1792].astype(BF16)
    pw['wg'] = jnp.pad(w[:, 1792:1816], ((0, 0), (0, LANES - 24))).astype(BF16)
    pw['gq'] = jnp.tile(g_q.reshape(1, HEAD_DIM), (1, 8))
    pw['gks'] = jnp.tile(g_k_sel.reshape(1, HEAD_DIM), (1, 2))
    pw['gkw'] = jnp.tile(g_k_win.reshape(1, HEAD_DIM), (1, 2))
    pw['ones512'] = _ones_bd(512)
    pw['ones128'] = _ones_bd(128)

    a_re, a_im = ssm_a_re.astype(F32), ssm_a_im.astype(F32)
    dt = jnp.exp(ssm_log_dt.astype(F32))[:, None]
    mag = jnp.exp(a_re * dt)
    ab_re = mag * jnp.cos(a_im * dt)
    ab_im = mag * jnp.sin(a_im * dt)
    den = a_re * a_re + a_im * a_im
    f_re = ((ab_re - 1.0) * a_re + ab_im * a_im) / den
    f_im = (ab_im * a_re - (ab_re - 1.0) * a_im) / den
    bb_re = f_re[..., None] * ssm_b_re - f_im[..., None] * ssm_b_im
    bb_im = f_re[..., None] * ssm_b_im + f_im[..., None] * ssm_b_re
    pw['ssm_ar'] = ab_re.reshape(N_PAIR, LANES)
    pw['ssm_ai'] = ab_im.reshape(N_PAIR, LANES)

    def in_block(bmat):
        m = bmat.reshape(4, 8, SSM_STATE, SSM_CH)
        eye = jnp.eye(8, dtype=F32)
        full = jnp.einsum('jgpc,gh->jgchp', m, eye)
        return full.reshape(4, 8 * SSM_CH, 8 * SSM_STATE)
    pw['ssm_wb'] = jnp.concatenate([in_block(bb_re), in_block(bb_im)], axis=2).astype(BF16)

    def out_block(cmat):
        m = cmat.reshape(4, 4, 2, SSM_CH, SSM_STATE)
        sel = np.zeros((4, 2, 8), np.float32)
        for pp in range(4):
            for e in range(2):
                sel[pp, e, 2 * pp + e] = 1.0
        full = jnp.einsum('jqecp,qeh->jqephc', m, jnp.asarray(sel))
        return full.reshape(N_PAIR, 2 * SSM_STATE, 8 * SSM_CH)
    pw['ssm_wcr'] = out_block(ssm_c_re.astype(F32)).astype(BF16)
    pw['ssm_wci'] = out_block(ssm_c_im.astype(F32)).astype(BF16)
    pw['ssm_d'] = ssm_d.reshape(1, SSM_WIDTH)
    pw['ssm_wglu'] = ssm_w_glu.astype(BF16)
    pw['ssm_bglu'] = ssm_b_glu.reshape(1, SSM_WIDTH)
    return pw


def kernel(x_prompt, x_sample, cache_k_cmp, cache_v_cmp, cache_k_sel, cache_v_sel, cache_k_win, cache_v_win, state_ssm_re, state_ssm_im, page_table, p_prompt, p_sample, g_mix, w_in, ssm_a_re, ssm_a_im, ssm_log_dt, ssm_b_re, ssm_b_im, ssm_c_re, ssm_c_im, ssm_d, ssm_w_glu, ssm_b_glu, g_q, g_k_cmp, g_k_sel, g_k_win, cmp_pe_k, cmp_w1_k, cmp_w2_k, cmp_pe_v, cmp_w1_v, cmp_w2_v, g_out_ssm, g_out_attn, w_out, g_ffn, peer_w_q, peer_subkeys, peer_u, peer_v, g_ple, ple_w_gate, ple_w_proj):
    bp, tp, _ = x_prompt.shape
    bs, ts, _ = x_sample.shape
    past_len = page_table.shape[1] * cache_k_cmp.shape[2]
    pw = _prep_weights(g_mix[0], w_in[0], ssm_a_re[0], ssm_a_im[0], ssm_log_dt[0], ssm_b_re[0], ssm_b_im[0],
                       ssm_c_re[0], ssm_c_im[0], ssm_d[0], ssm_w_glu[0], ssm_b_glu[0], g_q[0], g_k_sel[0], g_k_win[0])
    tabs_p = _rope_tables(jnp.arange(tp, dtype=jnp.int32))
    pos_s = past_len + jnp.arange(ts, dtype=jnp.int32)
    tabs_s = tuple(jnp.tile(a, (bs, 1)) for a in _rope_tables(pos_s))
    pp = _project(x_prompt.reshape(bp * tp, D_MODEL), tabs_p, tp // 512, 512, pw)
    ps = _project(x_sample.reshape(bs * ts, D_MODEL), tabs_s, 1, bs * ts, pw)
    zero_state = jnp.zeros((bp, N_PAIR, LANES), F32)
    yp, srp, sip = _ssm(pp[0].reshape(bp, tp, SSM_WIDTH), zero_state, zero_state, pw, bp, 128)
    ys, srs, sis = _ssm(ps[0].reshape(bs, ts, SSM_WIDTH), state_ssm_re[0].reshape(bs, N_PAIR, LANES),
                        state_ssm_im[0].reshape(bs, N_PAIR, LANES), pw, 8, ts)

    cw = {'gkc': jnp.tile(g_k_cmp[0].reshape(1, HEAD_DIM), (1, 2)), 'ones128': pw['ones128']}
    _compress_weights(cmp_pe_k[0], cmp_w1_k[0], cmp_w2_k[0], 'k', cw)
    _compress_weights(cmp_pe_v[0], cmp_w1_v[0], cmp_w2_v[0], 'v', cw)
    fw = _finish_weights(g_out_ssm[0], g_out_attn[0], w_out[0], g_ffn[0], peer_w_q[0], peer_subkeys[0], peer_u[0],
                         peer_v[0], g_ple[0], ple_w_gate[0], ple_w_proj[0])
    aw = _attn_consts(tp)
    n_pages = page_table.shape[1]
    sw = _sample_consts(n_pages)
    cmp_pos = lambda n: jnp.arange(n, dtype=jnp.int32) * CMP_STRIDE + (2 * CMP_STRIDE - 1)

    nblk = tp // CMP_STRIDE
    kcc, vcc = _compress_prompt(pp[2].reshape(bp, nblk, CMP_STRIDE * KV_WIDTH), pp[3].reshape(bp, nblk, CMP_STRIDE * KV_WIDTH),
                                cw, _rope_tables(cmp_pos(nblk)))
    att_p = _attn_prompt(pp[1], pp[12], kcc, vcc, pp[8], pp[9], pp[10], pp[11], aw, bp, tp, 256)
    hp = _finish(x_prompt.reshape(bp * tp, D_MODEL), yp.reshape(bp * tp, SSM_WIDTH), att_p,
                 p_prompt[0].reshape(bp * tp, -1), fw, 512, 512, 1024)

    ncp = n_pages * (PAGE // CMP_STRIDE) + 8
    att_s = _sample_attention(ps, page_table, cache_k_cmp[0], cache_v_cmp[0], cache_k_sel[0], cache_v_sel[0],
                              cache_k_win[0], cache_v_win[0], cw, _rope_tables(cmp_pos(ncp)), sw, aw['gexp'], past_len, bs)
    hs = _finish(x_sample.reshape(bs * ts, D_MODEL), ys.reshape(bs * ts, SSM_WIDTH), att_s.reshape(bs * ts, ATTN_WIDTH),
                 p_sample[0].reshape(bs * ts, -1), fw, bs * ts, bs * ts, 1024)

    rows = lambda a, b, t: a.reshape(1, b, t, N_KV, HEAD_DIM)
    state = lambda a, b: a.reshape(1, b, SSM_GROUPS, SSM_STATE)
    keep = min(WINDOW, tp)
    p_kw = rows(pp[6], bp, tp)[:, :, tp - keep:]
    p_vw = rows(pp[7], bp, tp)[:, :, tp - keep:]
    s_kw = jnp.concatenate([cache_k_win[0], rows(ps[6], bs, ts)[0]], axis=1)[None, :, -WINDOW:]
    s_vw = jnp.concatenate([cache_v_win[0], rows(ps[7], bs, ts)[0]], axis=1)[None, :, -WINDOW:]
    return (hp.reshape(bp, tp, D_MODEL), hs.reshape(bs, ts, D_MODEL),
            rows(pp[2], bp, tp), rows(pp[3], bp, tp), rows(pp[4], bp, tp), rows(pp[5], bp, tp), p_kw, p_vw,
            state(srp, bp), state(sip, bp),
            rows(ps[2], bs, ts), rows(ps[3], bs, ts), rows(ps[4], bs, ts), rows(ps[5], bs, ts), s_kw, s_vw,
            state(srs, bs), state(sis, bs))
```

```python
import functools
import math

import jax
import jax.numpy as jnp
import numpy as np
from jax import lax
from jax.experimental import pallas as pl
from jax.experimental.pallas import tpu as pltpu

F32 = jnp.float32
BF16 = jnp.bfloat16

D_MODEL = 1024
SSM_WIDTH = 512
SSM_CH = 16
SSM_GROUPS = 32
SSM_STATE = 64
ATTN_WIDTH = 512
HEAD_DIM = 64
N_HEADS = 8
N_KV = 2
GQA_REP = 4
KV_WIDTH = 128
ROPE_DIM = 16
ROPE_THETA = 500000.0
CMP_STRIDE = 16
CMP_HIDDEN = 128
SEL_BLOCK = 64
SEL_TOPN = 16
FORCE_BONUS = 1000.0
WINDOW = 512
PEER_HEADS = 8
PEER_KEYS = 128
PEER_TOPK = 16
RMS_EPS = 1e-6
NEG = -1e30
PAD_SCORE = -3.0e38
TAKEN_SCORE = -3.4e38
LANES = 128
VMEM_LIMIT = 56 * 1024 * 1024


def _dot(a, b):
    return jnp.dot(a, b, preferred_element_type=F32)


def _dot_nt(a, b):
    return lax.dot_general(a, b, (((1,), (1,)), ((), ())), preferred_element_type=F32)


def _split_dot(x, w):
    hi = x.astype(BF16)
    lo = (x - hi.astype(F32)).astype(BF16)
    return _dot(hi, w) + _dot(lo, w)


def _gelu(x):
    c = math.sqrt(2.0 / math.pi)
    return 0.5 * x * (1.0 + jnp.tanh(c * (x + 0.044715 * (x * x * x))))


def _sigmoid(x):
    return 1.0 / (1.0 + jnp.exp(-x))


def _tile_lanes(a, reps):
    return a if reps == 1 else jnp.concatenate([a] * reps, axis=1)


def _head_norm_rope(z, ones_bd, gain, c, s1, s2):
    width = z.shape[1]
    ss = _split_dot(z * z, ones_bd)
    zn = z * lax.rsqrt(ss * (1.0 / HEAD_DIM) + RMS_EPS) * gain
    reps = width // LANES
    half = ROPE_DIM // 2
    return (zn * _tile_lanes(c, reps)
            + pltpu.roll(zn, width - half, 1) * _tile_lanes(s1, reps)
            + pltpu.roll(zn, half, 1) * _tile_lanes(s2, reps))


def _cparams(sem):
    return pltpu.CompilerParams(dimension_semantics=sem, vmem_limit_bytes=VMEM_LIMIT)


def _project_kernel(x_ref, gmix_ref, wu_ref, wq_ref, wk_ref, wg_ref, gq_ref, gks_ref, gkw_ref,
                    c_ref, s1_ref, s2_ref, ones512_ref, ones128_ref,
                    u_ref, q_ref, kc_ref, vc_ref, ks_ref, vs_ref, kw_ref, vw_ref,
                    ksb_ref, vsb_ref, kwb_ref, vwb_ref, gate_ref):
    x = x_ref[...]
    a = x * lax.rsqrt(jnp.mean(x * x, axis=-1, keepdims=True) + RMS_EPS) * gmix_ref[...]
    ab = a.astype(BF16)
    c, s1, s2 = c_ref[...], s1_ref[...], s2_ref[...]
    u_ref[...] = _dot(ab, wu_ref[...])
    q = _head_norm_rope(_dot(ab, wq_ref[...]), ones512_ref[...], gq_ref[...], c, s1, s2)
    q_ref[...] = (q * (HEAD_DIM ** -0.5)).astype(BF16)
    zk = _dot(ab, wk_ref[...])
    kc_ref[...] = zk[:, 0:128]
    vc_ref[...] = zk[:, 128:256]
    ks = _head_norm_rope(zk[:, 256:384], ones128_ref[...], gks_ref[...], c, s1, s2)
    ks_ref[...] = ks
    ksb_ref[...] = ks.astype(BF16)
    vs = zk[:, 384:512]
    vs_ref[...] = vs
    vsb_ref[...] = vs.astype(BF16)
    kw = _head_norm_rope(zk[:, 512:640], ones128_ref[...], gkw_ref[...], c, s1, s2)
    kw_ref[...] = kw
    kwb_ref[...] = kw.astype(BF16)
    vw = zk[:, 640:768]
    vw_ref[...] = vw
    vwb_ref[...] = vw.astype(BF16)
    gate_ref[...] = _sigmoid(_dot(ab, wg_ref[...]))


def _project(x2d, rope_tabs, n_rope_tiles, tm, pw):
    n = x2d.shape[0]
    grid = (n // tm,)
    row = lambda w: pl.BlockSpec((tm, w), lambda i: (i, 0))
    full = lambda a: pl.BlockSpec(a.shape, lambda i: (0,) * a.ndim)
    rope_spec = pl.BlockSpec((tm, LANES), lambda i: (i % n_rope_tiles, 0))
    consts = [pw['gmix'], pw['wu'], pw['wq'], pw['wk'], pw['wg'], pw['gq'], pw['gks'], pw['gkw']]
    in_specs = [row(D_MODEL)] + [full(a) for a in consts] + [rope_spec] * 3 + [full(pw['ones512']), full(pw['ones128'])]
    out_shape = ([jax.ShapeDtypeStruct((n, 512), F32), jax.ShapeDtypeStruct((n, 512), BF16)]
                 + [jax.ShapeDtypeStruct((n, 128), F32)] * 6
                 + [jax.ShapeDtypeStruct((n, 128), BF16)] * 4
                 + [jax.ShapeDtypeStruct((n, 128), F32)])
    out_specs = [row(512), row(512)] + [row(128)] * 11
    return pl.pallas_call(
        _project_kernel, grid=grid, in_specs=in_specs, out_specs=out_specs, out_shape=out_shape,
        compiler_params=_cparams(("parallel",)), name="project",
    )(x2d, *consts, *rope_tabs, pw['ones512'], pw['ones128'])


N_PAIR = SSM_GROUPS // 2


def _ssm_kernel(u_ref, s0r_ref, s0i_ref, wb_ref, ar_ref, ai_ref, wcr_ref, wci_ref, d_ref, wglu_ref, bglu_ref,
                y_ref, sr_ref, si_ref, bre, bim, xr_s, xi_s, *, bb, tc):
    ci = pl.program_id(1)

    @pl.when(ci == 0)
    def _():
        xr_s[...] = s0r_ref[...]
        xi_s[...] = s0i_ref[...]

    u = u_ref[...].reshape(bb * tc, SSM_WIDTH)
    ub = u.astype(BF16)
    for j in range(4):
        z = _dot(ub[:, j * 128:(j + 1) * 128], wb_ref[j])
        for pp in range(4):
            pair = 4 * j + pp
            for b in range(bb):
                rows = slice(b * tc, (b + 1) * tc)
                bre[b, pl.ds(pair, tc, stride=N_PAIR), :] = z[rows, pp * 128:(pp + 1) * 128]
                bim[b, pl.ds(pair, tc, stride=N_PAIR), :] = z[rows, 512 + pp * 128:512 + (pp + 1) * 128]

    ar = ar_ref[...]
    ai = ai_ref[...]

    def step(t, carry):
        out = []
        off = pl.multiple_of(t * N_PAIR, N_PAIR)
        for b in range(bb):
            xr, xi = carry[2 * b], carry[2 * b + 1]
            nr = ar * xr - ai * xi + bre[b, pl.ds(off, N_PAIR), :]
            ni = ar * xi + ai * xr + bim[b, pl.ds(off, N_PAIR), :]
            bre[b, pl.ds(off, N_PAIR), :] = nr
            bim[b, pl.ds(off, N_PAIR), :] = ni
            out += [nr, ni]
        return tuple(out)

    init = []
    for b in range(bb):
        init += [xr_s[b], xi_s[b]]
    fin = lax.fori_loop(0, tc, step, tuple(init), unroll=8 if tc >= 8 else tc)
    for b in range(bb):
        xr_s[b] = fin[2 * b]
        xi_s[b] = fin[2 * b + 1]
    sr_ref[...] = xr_s[...]
    si_ref[...] = xi_s[...]

    ys = []
    for b in range(bb):
        tiles = []
        for j in range(4):
            acc = None
            for pp in range(4):
                pair = 4 * j + pp
                xr = bre[b, pl.ds(pair, tc, stride=N_PAIR), :].astype(BF16)
                xi = bim[b, pl.ds(pair, tc, stride=N_PAIR), :].astype(BF16)
                term = _dot(xr, wcr_ref[pair]) - _dot(xi, wci_ref[pair])
                acc = term if acc is None else acc + term
            tiles.append(acc)
        ys.append(jnp.concatenate(tiles, axis=1))
    y = jnp.concatenate(ys, axis=0) if bb > 1 else ys[0]
    y = y + d_ref[...] * u
    zg = _gelu(y)
    out = zg * _sigmoid(_dot(zg.astype(BF16), wglu_ref[...]) + bglu_ref[...])
    y_ref[...] = out.reshape(bb, tc, SSM_WIDTH)


def _ssm(u3, s0r, s0i, pw, bb, tc):
    bsz, t, _ = u3.shape
    grid = (bsz // bb, t // tc)
    full = lambda a: pl.BlockSpec(a.shape, lambda b, c: (0,) * a.ndim)
    st_spec = pl.BlockSpec((bb, N_PAIR, LANES), lambda b, c: (b, 0, 0))
    consts = [pw['ssm_wb'], pw['ssm_ar'], pw['ssm_ai'], pw['ssm_wcr'], pw['ssm_wci'], pw['ssm_d'], pw['ssm_wglu'], pw['ssm_bglu']]
    return pl.pallas_call(
        functools.partial(_ssm_kernel, bb=bb, tc=tc),
        grid=grid,
        in_specs=[pl.BlockSpec((bb, tc, SSM_WIDTH), lambda b, c: (b, c, 0)), st_spec, st_spec] + [full(a) for a in consts],
        out_specs=[pl.BlockSpec((bb, tc, SSM_WIDTH), lambda b, c: (b, c, 0)), st_spec, st_spec],
        out_shape=[jax.ShapeDtypeStruct((bsz, t, SSM_WIDTH), F32),
                   jax.ShapeDtypeStruct((bsz, N_PAIR, LANES), F32),
                   jax.ShapeDtypeStruct((bsz, N_PAIR, LANES), F32)],
        scratch_shapes=[pltpu.VMEM((bb, tc * N_PAIR, LANES), F32), pltpu.VMEM((bb, tc * N_PAIR, LANES), F32),
                        pltpu.VMEM((bb, N_PAIR, LANES), F32), pltpu.VMEM((bb, N_PAIR, LANES), F32)],
        compiler_params=_cparams(("parallel", "arbitrary")), name="ssm",
    )(u3, s0r, s0i, *consts)


def _compress_core(blk_b, w1big, peb, w2bd):
    h = _dot(blk_b, w1big)
    n = h.shape[0]
    pre = h[:, :256] + pltpu.roll(h[:, 256:], n - 1, 0) + peb
    return _dot(_gelu(pre).astype(BF16), w2bd)


def _pe_term(pe_ref, w1_ref):
    t = _dot(pe_ref[...], w1_ref[...])[0:1]
    return jnp.concatenate([t, t], axis=1)


def _compress_kernel(kr_ref, vr_ref, w1k_ref, w1v_ref, pek_ref, pev_ref, w1ko_ref, w1vo_ref, w2k_ref, w2v_ref,
                     gkc_ref, c_ref, s1_ref, s2_ref, ones128_ref, kc_ref, vc_ref):
    kc = _compress_core(kr_ref[0].astype(BF16), w1k_ref[...], _pe_term(pek_ref, w1ko_ref), w2k_ref[...])
    kc = _head_norm_rope(kc, ones128_ref[...], gkc_ref[...], c_ref[...], s1_ref[...], s2_ref[...])
    kc_ref[0] = kc.astype(BF16)
    vc = _compress_core(vr_ref[0].astype(BF16), w1v_ref[...], _pe_term(pev_ref, w1vo_ref), w2v_ref[...])
    vc_ref[0] = vc.astype(BF16)


def _compress_prompt(kc_rows, vc_rows, cw, tabs_c):
    bsz, nblk, _ = kc_rows.shape
    full = lambda a: pl.BlockSpec(a.shape, lambda b: (0,) * a.ndim)
    per_b = lambda w: pl.BlockSpec((1, nblk, w), lambda b: (b, 0, 0))
    consts = [cw['w1big_k'], cw['w1big_v'], cw['pe_k'], cw['pe_v'], cw['w1_k'], cw['w1_v'], cw['w2bd_k'], cw['w2bd_v'],
              cw['gkc'], *tabs_c, cw['ones128']]
    return pl.pallas_call(
        _compress_kernel, grid=(bsz,),
        in_specs=[per_b(2048), per_b(2048)] + [full(a) for a in consts],
        out_specs=[per_b(128), per_b(128)],
        out_shape=[jax.ShapeDtypeStruct((bsz, nblk, 128), BF16)] * 2,
        compiler_params=_cparams(("parallel",)), name="compress_prompt",
    )(kc_rows, vc_rows, *consts)


SEL_TK = 512


def _lane_tile4(a):
    return jnp.concatenate([a, a, a, a], axis=1)


def _top_blocks(score, lane_f, n_take):
    sel = jnp.zeros(score.shape, F32)
    for _ in range(n_take):
        mx = jnp.max(score, axis=1, keepdims=True)
        idx = jnp.min(jnp.where(score == mx, lane_f, 1e9), axis=1, keepdims=True)
        pick = lane_f == idx
        sel = jnp.where(pick, 1.0, sel)
        score = jnp.where(pick, TAKEN_SCORE, score)
    return sel


def _online_update(h, s, v_aug, acc_ref, m_ref):
    m_prev = m_ref[h]
    m_next = jnp.maximum(m_prev, jnp.max(s, axis=1, keepdims=True))
    reps = s.shape[1] // LANES
    p = jnp.exp(s - _tile_lanes(m_next, reps))
    alpha = jnp.exp(m_prev - m_next)
    acc_ref[h] = alpha * acc_ref[h] + _dot(p.astype(BF16), v_aug)
    m_ref[h] = m_next


def _attn_prompt_kernel(q_ref, gate_ref, kc_ref, vc_ref, ks_ref, vs_ref, kw_ref, vw_ref, pool_ref, exp_ref, gexp_ref,
                        o_ref, qh_ref, selb_ref, acc_ref, m_ref, *, tq, t_len):
    ncmp = t_len // CMP_STRIDE
    nsel = t_len // SEL_BLOCK
    qt = pl.program_id(1)
    q0 = qt * tq
    lane = lax.broadcasted_iota(jnp.int32, (tq, LANES), 1)
    lo = lane < HEAD_DIM
    qpos = q0 + lax.broadcasted_iota(jnp.int32, (tq, 1), 0)
    q = q_ref[0]
    zero_b = jnp.zeros((tq, LANES), BF16)
    for h in range(N_HEADS):
        j, half = h % 4, h // 4
        tile = q[:, j * LANES:(j + 1) * LANES]
        qh_ref[h] = jnp.where(lo if half == 0 else jnp.logical_not(lo), tile, zero_b)

    kc = kc_ref[0]
    vc = vc_ref[0]
    cidx = lax.broadcasted_iota(jnp.int32, (tq, ncmp), 1)
    cmask = (cidx * CMP_STRIDE + (2 * CMP_STRIDE - 1) <= qpos) & (cidx < ncmp - 1)
    o_cmp = []
    imp = [jnp.zeros((tq, ncmp), F32), jnp.zeros((tq, ncmp), F32)]
    for h in range(N_HEADS):
        s = jnp.where(cmask, _dot_nt(qh_ref[h], kc), NEG)
        mx = jnp.max(s, axis=1, keepdims=True)
        e = jnp.where(cmask, jnp.exp(s - mx), 0.0)
        l = jnp.sum(e, axis=1, keepdims=True)
        p = e / jnp.where(l > 0.0, l, 1.0)
        o_cmp.append(_dot(p.astype(BF16), vc))
        imp[h // GQA_REP] = imp[h // GQA_REP] + p

    cur = qpos // SEL_BLOCK
    forced = (lane == 0) | (lane == cur) | (lane == cur - 1)
    future = lane > cur
    lane_f = lane.astype(F32)
    for g in range(N_KV):
        sc = _split_dot(imp[g], pool_ref[...])
        sc = jnp.where(future, NEG, jnp.where(forced, sc + FORCE_BONUS, sc))
        sc = jnp.where(lane < nsel, sc, PAD_SCORE)
        selb_ref[g] = _top_blocks(sc, lane_f, min(SEL_TOPN, nsel)).astype(BF16)

    acc_ref[...] = jnp.zeros(acc_ref.shape, F32)
    m_ref[...] = jnp.full(m_ref.shape, NEG, F32)
    ones_b = jnp.ones((1, LANES), BF16)

    def aug_v(v, g, rows):
        lo_k = lax.broadcasted_iota(jnp.int32, (rows, LANES), 1) < HEAD_DIM
        keep = lo_k if g == 0 else jnp.logical_not(lo_k)
        return jnp.where(keep, v, jnp.broadcast_to(ones_b, (rows, LANES)))

    def sel_body(kt, carry):
        k0 = pl.multiple_of(kt * SEL_TK, SEL_TK)
        k = ks_ref[0, pl.ds(k0, SEL_TK), :]
        v = vs_ref[0, pl.ds(k0, SEL_TK), :]
        kpos = k0 + lax.broadcasted_iota(jnp.int32, (1, SEL_TK), 1)
        causal = kpos <= qpos
        for g in range(N_KV):
            mask = (_dot(selb_ref[g], exp_ref[kt]) > 0.5) & causal
            vg = aug_v(v, g, SEL_TK)
            for r in range(GQA_REP):
                h = GQA_REP * g + r
                s = jnp.where(mask, _dot_nt(qh_ref[h], k), NEG)
                _online_update(h, s, vg, acc_ref, m_ref)
        return carry

    n_kt = (q0 + tq + SEL_TK - 1) // SEL_TK
    lax.fori_loop(0, n_kt, sel_body, 0)

    def win_body(w, carry):
        k0 = pl.multiple_of((qt - w) * tq, tq)
        k = kw_ref[0, pl.ds(k0, tq), :]
        v = vw_ref[0, pl.ds(k0, tq), :]
        kpos = k0 + lax.broadcasted_iota(jnp.int32, (1, tq), 1)
        diff = qpos - kpos
        mask = (diff >= 0) & (diff < WINDOW)
        for g in range(N_KV):
            vg = aug_v(v, g, tq)
            for r in range(GQA_REP):
                h = GQA_REP * g + r
                s = jnp.where(mask, _dot_nt(qh_ref[h], k), NEG)
                _online_update(N_HEADS + h, s, vg, acc_ref, m_ref)
        return carry

    lax.fori_loop(0, jnp.minimum(qt, WINDOW // tq) + 1, win_body, 0)

    def normalised(idx):
        a = acc_ref[idx]
        return a / pltpu.roll(a, HEAD_DIM, 1)

    ge = _split_dot(gate_ref[...], gexp_ref[...])
    tiles = []
    for j in range(4):
        oc = jnp.where(lo, o_cmp[j], o_cmp[4 + j])
        osel = jnp.where(lo, normalised(j), normalised(4 + j))
        owin = jnp.where(lo, normalised(N_HEADS + j), normalised(N_HEADS + 4 + j))
        sl = slice(j * LANES, (j + 1) * LANES)
        tiles.append(ge[:, sl] * oc + ge[:, 512 + j * LANES:512 + (j + 1) * LANES] * osel
                     + ge[:, 1024 + j * LANES:1024 + (j + 1) * LANES] * owin)
    o_ref[...] = jnp.concatenate(tiles, axis=1)


def _attn_prompt(q, gate, kc, vc, ksb, vsb, kwb, vwb, aw, bsz, t_len, tq):
    nq = t_len // tq
    ncmp = t_len // CMP_STRIDE
    full = lambda a: pl.BlockSpec(a.shape, lambda b, i: (0,) * a.ndim)
    tok = lambda w: pl.BlockSpec((tq, w), lambda b, i: (b * nq + i, 0))
    per_b_q = pl.BlockSpec((1, tq, 512), lambda b, i: (b, i, 0))
    per_b = lambda n, w: pl.BlockSpec((1, n, w), lambda b, i: (b, 0, 0))
    return pl.pallas_call(
        functools.partial(_attn_prompt_kernel, tq=tq, t_len=t_len),
        grid=(bsz, nq),
        in_specs=[per_b_q, tok(LANES), per_b(ncmp, LANES), per_b(ncmp, LANES)] + [per_b(t_len, LANES)] * 4
                 + [full(aw['pool']), full(aw['expand']), full(aw['gexp'])],
        out_specs=tok(ATTN_WIDTH),
        out_shape=jax.ShapeDtypeStruct((bsz * t_len, ATTN_WIDTH), F32),
        scratch_shapes=[pltpu.VMEM((N_HEADS, tq, LANES), BF16), pltpu.VMEM((N_KV, tq, LANES), BF16),
                        pltpu.VMEM((2 * N_HEADS, tq, LANES), F32), pltpu.VMEM((2 * N_HEADS, tq, LANES), F32)],
        compiler_params=_cparams(("parallel", "arbitrary")), name="attn_prompt",
    )(q.reshape(bsz, t_len, 512), gate, kc, vc, ksb.reshape(bsz, t_len, LANES), vsb.reshape(bsz, t_len, LANES),
      kwb.reshape(bsz, t_len, LANES), vwb.reshape(bsz, t_len, LANES), aw['pool'], aw['expand'], aw['gexp'])


def _attn_consts(t_len):
    ncmp = t_len // CMP_STRIDE
    c = np.arange(ncmp)
    pool = (c[:, None] // (SEL_BLOCK // CMP_STRIDE) == np.arange(LANES)[None, :])
    n_kt = max(t_len // SEL_TK, 1)
    key = np.arange(n_kt * SEL_TK).reshape(n_kt, 1, SEL_TK)
    expand = (key // SEL_BLOCK == np.arange(LANES)[None, :, None])
    perm_head = _q_perm() // HEAD_DIM
    rows = np.arange(LANES)[:, None]
    gexp = np.concatenate([rows == (br * N_HEADS + perm_head)[None, :] for br in range(3)], axis=1)
    return {'pool': jnp.asarray(pool, BF16), 'expand': jnp.asarray(expand, BF16), 'gexp': jnp.asarray(gexp, BF16)}


def _compress_weights(pe, w1, w2, tag, cw):
    w1r = w1.reshape(2, CMP_STRIDE, HEAD_DIM, CMP_HIDDEN)
    eye = jnp.eye(N_KV, dtype=F32)
    big = jnp.einsum('srdh,ge->rgdseh', w1r, eye).reshape(CMP_STRIDE * KV_WIDTH, 2 * N_KV * CMP_HIDDEN)
    cw['w1big_' + tag] = big.astype(BF16)
    cw['w1_' + tag] = w1.astype(BF16)
    cw['pe_' + tag] = jnp.tile(pe.reshape(1, -1), (8, 1)).astype(BF16)
    cw['w2bd_' + tag] = jnp.einsum('hd,ge->ghed', w2, eye).reshape(N_KV * CMP_HIDDEN, KV_WIDTH).astype(BF16)


PAGE = 128
CHUNK_PAGES = 32
CHUNK_ROWS = CHUNK_PAGES * PAGE
DEC_Q = 8


def _start_pages(pt_ref, b, c, pools, bufs, sem, slot, dst_of):
    def body(p, carry):
        page = pt_ref[b, c * CHUNK_PAGES + p]
        for i, (pool, buf) in enumerate(zip(pools, bufs)):
            pltpu.make_async_copy(pool.at[page], dst_of(buf, slot, p), sem.at[i, slot]).start()
        return carry
    lax.fori_loop(0, CHUNK_PAGES, body, 0)


def _wait_pages(pools, bufs, sem, slot, dst_of):
    def body(p, carry):
        for i, (pool, buf) in enumerate(zip(pools, bufs)):
            pltpu.make_async_copy(pool.at[0], dst_of(buf, slot, p), sem.at[i, slot]).wait()
        return carry
    lax.fori_loop(0, CHUNK_PAGES, body, 0)


def _gather_step(pt_ref, pools, bufs, sem, dst_of, n_chunks):
    b, c = pl.program_id(0), pl.program_id(1)
    step = b * n_chunks + c
    slot = step % 2
    total = pl.num_programs(0) * n_chunks

    @pl.when(step == 0)
    def _():
        _start_pages(pt_ref, 0, 0, pools, bufs, sem, 0, dst_of)

    @pl.when(step + 1 < total)
    def _():
        nxt = step + 1
        _start_pages(pt_ref, nxt // n_chunks, nxt % n_chunks, pools, bufs, sem, 1 - slot, dst_of)

    _wait_pages(pools, bufs, sem, slot, dst_of)
    return slot


def _cols_dst(buf, slot, p):
    return buf.at[slot, :, pl.ds(p * PAGE, PAGE)]


def _lanes_dst(buf, slot, p):
    return buf.at[slot, :, :, pl.ds(p * PAGE, PAGE)]


def _stack_heads_masked(q):
    lo = lax.broadcasted_iota(jnp.int32, (DEC_Q, LANES), 1) < HEAD_DIM
    zero = jnp.zeros((DEC_Q, LANES), q.dtype)
    rows = []
    for h in range(N_HEADS):
        j, half = h % 4, h // 4
        tile = q[:, j * LANES:(j + 1) * LANES]
        rows.append(jnp.where(lo if half == 0 else jnp.logical_not(lo), tile, zero))
    return jnp.concatenate(rows, axis=0)


def _sample_cmp_kernel(pt_ref, kpool, vpool, knew_ref, vnew_ref, q_ref, w1k_ref, w1v_ref, pek_ref, pev_ref,
                       w1ko_ref, w1vo_ref, w2k_ref, w2v_ref, gkc_ref, c_ref, s1_ref, s2_ref, ones128_ref,
                       pool_ref, blkid_ref, ocmp_ref, sel_ref, kbuf, vbuf, sem, hk, hv, rows_ref, *, n_chunks, past_len):
    c = pl.program_id(1)
    slot = _gather_step(pt_ref, (kpool, vpool), (kbuf, vbuf), sem, _cols_dst, n_chunks)
    nhb_chunk = CHUNK_ROWS // CMP_STRIDE
    tr = 512
    for buf, w1_ref, h_ref in ((kbuf, w1k_ref, hk), (vbuf, w1v_ref, hv)):
        for t in range(CHUNK_ROWS // tr):
            rows_ref[t * tr:(t + 1) * tr, :] = buf[slot, :, t * tr:(t + 1) * tr].T
        acc = None
        for r in range(CMP_STRIDE):
            rows = rows_ref[pl.ds(r, nhb_chunk, stride=CMP_STRIDE), :].astype(BF16)
            t = _dot(rows, w1_ref[r])
            acc = t if acc is None else acc + t
        h_ref[pl.ds(pl.multiple_of(c * nhb_chunk, nhb_chunk), nhb_chunk), :] = acc

    @pl.when(c == n_chunks - 1)
    def _():
        nhb = n_chunks * nhb_chunk
        ncp = nhb + 8
        n_valid = nhb + 3
        for new_ref, w1_ref, h_ref in ((knew_ref, w1k_ref, hk), (vnew_ref, w1v_ref, hv)):
            new_b = new_ref[0].astype(BF16)
            hn = None
            for r in range(DEC_Q):
                t = _dot(new_b, w1_ref[r])[r:r + 1]
                hn = t if hn is None else hn + t
            row = lax.broadcasted_iota(jnp.int32, (8, 4 * CMP_HIDDEN), 0)
            h_ref[pl.ds(nhb, 8), :] = jnp.where(row == 0, jnp.broadcast_to(hn, (8, 4 * CMP_HIDDEN)), 0.0)

        def mlp(h_ref, pe_ref, w1o_ref, w2_ref):
            h = h_ref[...]
            pre = h[:, :256] + pltpu.roll(h[:, 256:], ncp - 1, 0) + _pe_term(pe_ref, w1o_ref)
            return _dot(_gelu(pre).astype(BF16), w2_ref[...])

        kc = mlp(hk, pek_ref, w1ko_ref, w2k_ref)
        kc = _head_norm_rope(kc, ones128_ref[...], gkc_ref[...], c_ref[...], s1_ref[...], s2_ref[...]).astype(BF16)
        vc = mlp(hv, pev_ref, w1vo_ref, w2v_ref).astype(BF16)

        qm = _stack_heads_masked(q_ref[0])
        rowq = lax.broadcasted_iota(jnp.int32, (N_HEADS * DEC_Q, 1), 0) % DEC_Q
        qpos = past_len + rowq
        cidx = lax.broadcasted_iota(jnp.int32, (N_HEADS * DEC_Q, ncp), 1)
        cmask = (cidx * CMP_STRIDE + (2 * CMP_STRIDE - 1) <= qpos) & (cidx < n_valid)
        s = jnp.where(cmask, _dot_nt(qm, kc), NEG)
        mx = jnp.max(s, axis=1, keepdims=True)
        e = jnp.where(cmask, jnp.exp(s - mx), 0.0)
        l = jnp.sum(e, axis=1, keepdims=True)
        p = e / jnp.where(l > 0.0, l, 1.0)
        ocmp_ref[0] = _dot(p.astype(BF16), vc)

        imps = []
        for g in range(N_KV):
            base = g * GQA_REP * DEC_Q
            acc = p[base:base + DEC_Q]
            for r in range(1, GQA_REP):
                acc = acc + p[base + r * DEC_Q:base + (r + 1) * DEC_Q]
            imps.append(acc)
        imp = jnp.concatenate(imps, axis=0)
        sc = _split_dot(imp, pool_ref[...])
        nsl = sc.shape[1]
        blk = jnp.broadcast_to(blkid_ref[...], (N_KV * DEC_Q, nsl))
        cur = past_len // SEL_BLOCK
        forced = (blk == 0) | (blk == cur) | (blk == cur - 1)
        future = blk > cur
        sc = jnp.where(future, NEG, jnp.where(forced, sc + FORCE_BONUS, sc))
        sc = jnp.where(blk >= 0, sc, PAD_SCORE)
        lane_f = lax.broadcasted_iota(jnp.int32, (N_KV * DEC_Q, nsl), 1).astype(F32)
        sel = _top_blocks(sc, lane_f, SEL_TOPN).astype(BF16)
        for t in range(nsl // LANES):
            sel_ref[0, t] = sel[:, t * LANES:(t + 1) * LANES]


def _sample_cmp(page_table, kpool, vpool, knew, vnew, q, cw, tabs_c, sw, past_len):
    bsz, n_pages = page_table.shape
    n_chunks = n_pages // CHUNK_PAGES
    ncp = n_pages * (PAGE // CMP_STRIDE) + 8
    nsl = (n_chunks + 1) * LANES
    full = lambda a: pl.BlockSpec(a.shape, lambda b, c, pt: (0,) * a.ndim)
    per_b = lambda s1, s2: pl.BlockSpec((1, s1, s2), lambda b, c, pt: (b, 0, 0))
    any_spec = pl.BlockSpec(memory_space=pl.ANY)
    w1k = cw['w1big_k'].reshape(CMP_STRIDE, KV_WIDTH, 4 * CMP_HIDDEN)
    w1v = cw['w1big_v'].reshape(CMP_STRIDE, KV_WIDTH, 4 * CMP_HIDDEN)
    consts = [w1k, w1v, cw['pe_k'], cw['pe_v'], cw['w1_k'], cw['w1_v'], cw['w2bd_k'], cw['w2bd_v'],
              cw['gkc'], *tabs_c, cw['ones128'], sw['pool'], sw['blkid']]
    grid_spec = pltpu.PrefetchScalarGridSpec(
        num_scalar_prefetch=1, grid=(bsz, n_chunks),
        in_specs=[any_spec, any_spec, per_b(DEC_Q, LANES), per_b(DEC_Q, LANES), per_b(DEC_Q, 512)] + [full(a) for a in consts],
        out_specs=[per_b(N_HEADS * DEC_Q, LANES),
                   pl.BlockSpec((1, n_chunks + 1, N_KV * DEC_Q, LANES), lambda b, c, pt: (b, 0, 0, 0))],
        scratch_shapes=[pltpu.VMEM((2, KV_WIDTH, CHUNK_ROWS), F32), pltpu.VMEM((2, KV_WIDTH, CHUNK_ROWS), F32),
                        pltpu.SemaphoreType.DMA((2, 2)),
                        pltpu.VMEM((ncp, 4 * CMP_HIDDEN), F32), pltpu.VMEM((ncp, 4 * CMP_HIDDEN), F32),
                        pltpu.VMEM((CHUNK_ROWS, KV_WIDTH), F32)])
    return pl.pallas_call(
        functools.partial(_sample_cmp_kernel, n_chunks=n_chunks, past_len=past_len),
        grid_spec=grid_spec,
        out_shape=[jax.ShapeDtypeStruct((bsz, N_HEADS * DEC_Q, LANES), F32),
                   jax.ShapeDtypeStruct((bsz, n_chunks + 1, N_KV * DEC_Q, LANES), BF16)],
        compiler_params=_cparams(("arbitrary", "arbitrary")), name="sample_cmp",
    )(page_table, kpool, vpool, knew, vnew, q, *consts)


def _sample_consts(n_pages):
    n_chunks = n_pages // CHUNK_PAGES
    nsl = (n_chunks + 1) * LANES
    ns = n_pages * (PAGE // SEL_BLOCK) + 1
    lane = np.arange(nsl)
    blk = (lane // LANES) * (CHUNK_ROWS // SEL_BLOCK) + lane % LANES
    blkid = np.where((lane % LANES < CHUNK_ROWS // SEL_BLOCK) & (blk < ns), blk, -1).astype(np.int32)
    ncp = n_pages * (PAGE // CMP_STRIDE) + 8
    pool = (np.arange(ncp)[:, None] // (SEL_BLOCK // CMP_STRIDE)) == blkid[None, :]
    key = np.arange(CHUNK_ROWS)
    expand = (key[None, :] // SEL_BLOCK == np.arange(LANES)[:, None])
    return {'pool': jnp.asarray(pool, BF16), 'blkid': jnp.asarray(blkid.reshape(1, nsl)),
            'expand': jnp.asarray(expand, BF16)}


def _flash_rows(s, mask, vt, m_ref, l_ref, acc_ref, idx):
    s = jnp.where(mask, s, NEG)
    m_prev = m_ref[idx]
    m_next = jnp.maximum(m_prev, jnp.max(s, axis=1, keepdims=True))
    p = jnp.exp(s - m_next[:, 0:1])
    alpha = jnp.exp(m_prev - m_next)
    l_ref[idx] = alpha * l_ref[idx] + jnp.sum(p, axis=1, keepdims=True)
    acc_ref[idx] = alpha[:, :HEAD_DIM] * acc_ref[idx] + _dot_nt(p.astype(BF16), vt)
    m_ref[idx] = m_next


def _sample_sel_kernel(pt_ref, kpool, vpool, q_ref, ksn_ref, vsn_ref, kwt_ref, vwt_ref, kwn_ref, vwn_ref,
                       sel_ref, ocmp_ref, gate_ref, exp_ref, gexp_ref, o_ref,
                       kbuf, vbuf, sem, m_ref, l_ref, acc_ref, *, n_chunks, past_len):
    c = pl.program_id(1)
    slot = _gather_step(pt_ref, (kpool, vpool), (kbuf, vbuf), sem, _lanes_dst, n_chunks)
    rows = GQA_REP * DEC_Q
    q = q_ref[0]

    def group_q(g):
        return jnp.concatenate([q[:, r * LANES + g * HEAD_DIM:r * LANES + (g + 1) * HEAD_DIM] for r in range(GQA_REP)], axis=0)

    @pl.when(c == 0)
    def _():
        m_ref[...] = jnp.full(m_ref.shape, NEG, F32)
        l_ref[...] = jnp.zeros(l_ref.shape, F32)
        acc_ref[...] = jnp.zeros(acc_ref.shape, F32)

    def four(a):
        return jnp.concatenate([a] * GQA_REP, axis=0)

    mexp = _dot(sel_ref[0, c], exp_ref[...]) > 0.5
    for g in range(N_KV):
        kt = kbuf[slot, g].astype(BF16)
        vt = vbuf[slot, g].astype(BF16)
        s = _dot(group_q(g), kt)
        _flash_rows(s, four(mexp[g * DEC_Q:(g + 1) * DEC_Q]), vt, m_ref, l_ref, acc_ref, g)

    @pl.when(c == n_chunks - 1)
    def _():
        qm = _stack_heads_masked(q)
        rowq = lax.broadcasted_iota(jnp.int32, (N_HEADS * DEC_Q, 1), 0) % DEC_Q
        col = lax.broadcasted_iota(jnp.int32, (N_HEADS * DEC_Q, LANES), 1)
        new_mask = col <= rowq
        lo = col < HEAD_DIM

        def new_keys_step(kn_ref, vn_ref, base, extra_mask):
            s = _dot_nt(qm, kn_ref[0])
            msk = new_mask & extra_mask
            for g in range(N_KV):
                sl = slice(g * rows, (g + 1) * rows)
                vt_full = vn_ref[0]
                vt = vt_full[:, g * HEAD_DIM:(g + 1) * HEAD_DIM]
                sg = jnp.where(msk[sl], s[sl], NEG)
                idx = base + g
                m_prev = m_ref[idx]
                m_next = jnp.maximum(m_prev, jnp.max(sg, axis=1, keepdims=True))
                p = jnp.exp(sg - m_next[:, 0:1])
                alpha = jnp.exp(m_prev - m_next)
                l_ref[idx] = alpha * l_ref[idx] + jnp.sum(p, axis=1, keepdims=True)
                acc_ref[idx] = alpha[:, :HEAD_DIM] * acc_ref[idx] + _dot(p.astype(BF16), vt)
                m_ref[idx] = m_next

        sel_new = sel_ref[0, n_chunks][:, 0:1].astype(F32) > 0.5
        sel_rows = jnp.concatenate([four(sel_new[g * DEC_Q:(g + 1) * DEC_Q]) for g in range(N_KV)], axis=0)
        new_keys_step(ksn_ref, vsn_ref, 0, sel_rows)

        wcol = lax.broadcasted_iota(jnp.int32, (rows, WINDOW), 1)
        wq = lax.broadcasted_iota(jnp.int32, (rows, 1), 0) % DEC_Q
        wmask = wcol > wq
        for g in range(N_KV):
            kt = kwt_ref[0, g].astype(BF16)
            vt = vwt_ref[0, g].astype(BF16)
            _flash_rows(_dot(group_q(g), kt), wmask, vt, m_ref, l_ref, acc_ref, N_KV + g)
        new_keys_step(kwn_ref, vwn_ref, N_KV, jnp.full((N_HEADS * DEC_Q, 1), True))

        ge = _split_dot(gate_ref[0], gexp_ref[...])
        oc = ocmp_ref[0]
        lo8 = lo[:DEC_Q]
        tiles = []
        for r in range(GQA_REP):
            rs = slice(r * DEC_Q, (r + 1) * DEC_Q)
            osel = jnp.concatenate([acc_ref[g][rs] / l_ref[g][rs, 0:1] for g in range(N_KV)], axis=1)
            owin = jnp.concatenate([acc_ref[N_KV + g][rs] / l_ref[N_KV + g][rs, 0:1] for g in range(N_KV)], axis=1)
            ocr = jnp.where(lo8, oc[rs], oc[GQA_REP * DEC_Q + r * DEC_Q:GQA_REP * DEC_Q + (r + 1) * DEC_Q])
            tiles.append(ge[:, r * LANES:(r + 1) * LANES] * ocr + ge[:, 512 + r * LANES:512 + (r + 1) * LANES] * osel
                         + ge[:, 1024 + r * LANES:1024 + (r + 1) * LANES] * owin)
        o_ref[0] = jnp.concatenate(tiles, axis=1)


def _sample_sel(page_table, kpool_t, vpool_t, q, ksn, vsn, kwt, vwt, kwn, vwn, sel, ocmp, gate, sw, gexp, past_len):
    bsz, n_pages = page_table.shape
    n_chunks = n_pages // CHUNK_PAGES
    full = lambda a: pl.BlockSpec(a.shape, lambda b, c, pt: (0,) * a.ndim)
    per_b = lambda *s: pl.BlockSpec((1,) + tuple(s), lambda b, c, pt: (b,) + (0,) * len(s))
    any_spec = pl.BlockSpec(memory_space=pl.ANY)
    rows = GQA_REP * DEC_Q
    grid_spec = pltpu.PrefetchScalarGridSpec(
        num_scalar_prefetch=1, grid=(bsz, n_chunks),
        in_specs=[any_spec, any_spec, per_b(DEC_Q, 512), per_b(LANES, LANES), per_b(LANES, LANES),
                  per_b(N_KV, HEAD_DIM, WINDOW), per_b(N_KV, HEAD_DIM, WINDOW), per_b(LANES, LANES), per_b(LANES, LANES),
                  per_b(n_chunks + 1, N_KV * DEC_Q, LANES), per_b(N_HEADS * DEC_Q, LANES), per_b(DEC_Q, LANES),
                  full(sw['expand']), full(gexp)],
        out_specs=per_b(DEC_Q, ATTN_WIDTH),
        scratch_shapes=[pltpu.VMEM((2, N_KV, HEAD_DIM, CHUNK_ROWS), F32), pltpu.VMEM((2, N_KV, HEAD_DIM, CHUNK_ROWS), F32),
                        pltpu.SemaphoreType.DMA((2, 2)),
                        pltpu.VMEM((2 * N_KV, rows, LANES), F32), pltpu.VMEM((2 * N_KV, rows, LANES), F32),
                        pltpu.VMEM((2 * N_KV, rows, HEAD_DIM), F32)])
    return pl.pallas_call(
        functools.partial(_sample_sel_kernel, n_chunks=n_chunks, past_len=past_len),
        grid_spec=grid_spec,
        out_shape=jax.ShapeDtypeStruct((bsz, DEC_Q, ATTN_WIDTH), F32),
        compiler_params=_cparams(("arbitrary", "arbitrary")), name="sample_sel",
    )(page_table, kpool_t, vpool_t, q, ksn, vsn, kwt, vwt, kwn, vwn, sel, ocmp, gate, sw['expand'], gexp)


def _rms(x, g):
    return x * lax.rsqrt(jnp.mean(x * x, axis=-1, keepdims=True) + RMS_EPS) * g


def _outproj_kernel(x_ref, y_ref, a_ref, gos_ref, goa_ref, wa_ref, wb_ref, gffn_ref, h_ref, xn_ref):
    mixed = (_dot(_rms(y_ref[...], gos_ref[...]).astype(BF16), wa_ref[...])
             + _dot(_rms(a_ref[...], goa_ref[...]).astype(BF16), wb_ref[...]))
    h = x_ref[...] + mixed
    h_ref[...] = h
    xn_ref[...] = _rms(h, gffn_ref[...]).astype(BF16)


def _outproj(x2d, y, a, fw, tm):
    n = x2d.shape[0]
    row = lambda w: pl.BlockSpec((tm, w), lambda i: (i, 0))
    full = lambda arr: pl.BlockSpec(arr.shape, lambda i: (0,) * arr.ndim)
    consts = [fw['gos'], fw['goa'], fw['wout_a'], fw['wout_b'], fw['gffn']]
    return pl.pallas_call(
        _outproj_kernel, grid=(n // tm,),
        in_specs=[row(D_MODEL), row(512), row(512)] + [full(c) for c in consts],
        out_specs=[row(D_MODEL), row(D_MODEL)],
        out_shape=[jax.ShapeDtypeStruct((n, D_MODEL), F32), jax.ShapeDtypeStruct((n, D_MODEL), BF16)],
        compiler_params=_cparams(("parallel",)), name="outproj",
    )(x2d, y, a, *consts)


def _cex(a, b):
    return jnp.maximum(a, b), jnp.minimum(a, b)


def _bitonic_merge_desc(x):
    n = len(x)
    j = n // 2
    while j >= 1:
        for i in range(n):
            l = i ^ j
            if l > i:
                x[i], x[l] = _cex(x[i], x[l])
        j //= 2
    return x


def _bitonic_sort_desc(x):
    n = len(x)
    k = 2
    while k <= n:
        j = k // 2
        while j >= 1:
            for i in range(n):
                l = i ^ j
                if l > i:
                    hi, lo = _cex(x[i], x[l])
                    x[i], x[l] = (hi, lo) if (i & k) == 0 else (lo, hi)
            j //= 2
        k *= 2
    return x


def _merge_sublanes_top(x):
    n = len(x)
    for shift in (4, 2, 1):
        other = [pltpu.roll(a, shift, 0) for a in x]
        x = _bitonic_merge_desc([jnp.maximum(x[i], other[n - 1 - i]) for i in range(n)])
    return x


def _peer_select_kernel(xn_ref, wqt_ref, sub_ref, s_ref, st_ref, *, tm):
    qt = _dot_nt(wqt_ref[...], xn_ref[...]).astype(BF16)
    sub8 = lax.broadcasted_iota(jnp.int32, (8, tm), 0)
    tops = []
    for hs in range(2 * PEER_HEADS):
        s = _dot(sub_ref[hs], qt[hs * PEER_KEYS:(hs + 1) * PEER_KEYS])
        s_ref[hs] = s
        x = _bitonic_sort_desc([s[8 * v:8 * v + 8] for v in range(PEER_TOPK)])
        tops.append(_merge_sublanes_top(x))
    for h in range(PEER_HEADS):
        v0, v1 = tops[2 * h], tops[2 * h + 1]
        lo = jnp.zeros((8, tm), F32)
        hi = jnp.zeros((8, tm), F32)
        for b in range(8):
            lo = jnp.where(sub8 == b, v1[b], lo)
            hi = jnp.where(sub8 == b, v1[8 + b], hi)
        l_lo = [v0[a] + lo for a in range(PEER_TOPK)]
        l_hi = [v0[a] + hi for a in range(PEER_TOPK)]
        c = _bitonic_merge_desc([jnp.maximum(l_lo[i], l_hi[PEER_TOPK - 1 - i]) for i in range(PEER_TOPK)])
        c = _merge_sublanes_top(c)
        z = jnp.zeros((8, tm), F32)
        for r in range(PEER_TOPK):
            z = z + jnp.exp(c[r] - c[0])
        row = lax.broadcasted_iota(jnp.int32, (8, tm), 0)
        st = jnp.where(row == 0, c[PEER_TOPK - 1],
                       jnp.where(row == 1, v0[0], jnp.where(row == 2, v1[0], jnp.where(row == 3, 1.0 / z, 0.0))))
        st_ref[h] = st


def _peer_select(xn, wqt, sub, tm):
    n = xn.shape[0]
    return pl.pallas_call(
        functools.partial(_peer_select_kernel, tm=tm), grid=(n // tm,),
        in_specs=[pl.BlockSpec((tm, D_MODEL), lambda i: (i, 0)),
                  pl.BlockSpec(wqt.shape, lambda i: (0, 0)), pl.BlockSpec(sub.shape, lambda i: (0, 0, 0))],
        out_specs=[pl.BlockSpec((2 * PEER_HEADS, PEER_KEYS, tm), lambda i: (0, 0, i)),
                   pl.BlockSpec((PEER_HEADS, 8, tm), lambda i: (0, 0, i))],
        out_shape=[jax.ShapeDtypeStruct((2 * PEER_HEADS, PEER_KEYS, n), F32),
                   jax.ShapeDtypeStruct((PEER_HEADS, 8, n), F32)],
        compiler_params=_cparams(("parallel",)), name="peer_select",
    )(xn, wqt, sub)


def _peer_dense_kernel(xn_ref, u_ref, vt_ref, s_ref, st_ref, h_ref, o_ref, a0_ref, p1_ref, coef_ref, acc_ref, *, tn, te):
    j = pl.program_id(1)

    @pl.when(j == 0)
    def _():
        acc_ref[...] = jnp.zeros(acc_ref.shape, F32)
        for h in range(PEER_HEADS):
            st = st_ref[h]
            a0_ref[h] = jnp.exp(s_ref[2 * h] - st[1:2]) * st[3:4]
            p1_ref[h] = jnp.exp(s_ref[2 * h + 1] - st[2:3])

    act = _dot_nt(u_ref[...], xn_ref[...])
    n_i = te // PEER_KEYS
    i0 = pl.multiple_of(j * n_i, n_i)
    for ii in range(n_i):
        for lt in range(tn // LANES):
            ls = slice(lt * LANES, (lt + 1) * LANES)
            w = jnp.zeros((PEER_KEYS, LANES), F32)
            for h in range(PEER_HEADS):
                s0 = s_ref[2 * h, pl.ds(i0, n_i), ls][ii:ii + 1]
                a0 = a0_ref[h, pl.ds(i0, n_i), ls][ii:ii + 1]
                thr = st_ref[h, 0:1, ls]
                hit = (s_ref[2 * h + 1, :, ls] + s0) >= thr
                w = w + jnp.where(hit, p1_ref[h, :, ls], 0.0) * a0
            coef_ref[ii * PEER_KEYS:(ii + 1) * PEER_KEYS, ls] = (w * _gelu(act[ii * PEER_KEYS:(ii + 1) * PEER_KEYS, ls])).astype(BF16)
    acc_ref[...] += _dot(vt_ref[...], coef_ref[...])

    @pl.when(j == pl.num_programs(1) - 1)
    def _():
        o_ref[...] = h_ref[...] + acc_ref[...].T


def _peer_dense(xn, u_b, vt_b, s_all, stats, h1, tn, te):
    n = xn.shape[0]
    n_exp = u_b.shape[0]
    return pl.pallas_call(
        functools.partial(_peer_dense_kernel, tn=tn, te=te), grid=(n // tn, n_exp // te),
        in_specs=[pl.BlockSpec((tn, D_MODEL), lambda i, j: (i, 0)),
                  pl.BlockSpec((te, D_MODEL), lambda i, j: (j, 0)),
                  pl.BlockSpec((D_MODEL, te), lambda i, j: (0, j)),
                  pl.BlockSpec((2 * PEER_HEADS, PEER_KEYS, tn), lambda i, j: (0, 0, i)),
                  pl.BlockSpec((PEER_HEADS, 8, tn), lambda i, j: (0, 0, i)),
                  pl.BlockSpec((tn, D_MODEL), lambda i, j: (i, 0))],
        out_specs=pl.BlockSpec((tn, D_MODEL), lambda i, j: (i, 0)),
        out_shape=jax.ShapeDtypeStruct((n, D_MODEL), F32),
        scratch_shapes=[pltpu.VMEM((PEER_HEADS, PEER_KEYS, tn), F32), pltpu.VMEM((PEER_HEADS, PEER_KEYS, tn), F32),
                        pltpu.VMEM((te, tn), BF16), pltpu.VMEM((D_MODEL, tn), F32)],
        compiler_params=_cparams(("parallel", "arbitrary")), name="peer_dense",
    )(xn, u_b, vt_b, s_all, stats, h1)


def _ple_kernel(h_ref, p_ref, gple_ref, wg_ref, wp_ref, o_ref):
    h = h_ref[...]
    gate = _sigmoid(_dot(_rms(h, gple_ref[...]).astype(BF16), wg_ref[...]))
    o_ref[...] = h + gate * _dot(p_ref[...].astype(BF16), wp_ref[...])


def _ple(h2, p2d, fw, tm):
    n = h2.shape[0]
    row = lambda w: pl.BlockSpec((tm, w), lambda i: (i, 0))
    full = lambda arr: pl.BlockSpec(arr.shape, lambda i: (0,) * arr.ndim)
    consts = [fw['gple'], fw['wgate'], fw['wproj']]
    return pl.pallas_call(
        _ple_kernel, grid=(n // tm,),
        in_specs=[row(D_MODEL), row(p2d.shape[1])] + [full(c) for c in consts],
        out_specs=row(D_MODEL), out_shape=jax.ShapeDtypeStruct((n, D_MODEL), F32),
        compiler_params=_cparams(("parallel",)), name="ple",
    )(h2, p2d, *consts)


def _finish(x2d, ssm_y, attn_o, ple2d, fw, tm, tn, te):
    h1, xn = _outproj(x2d, ssm_y, attn_o, fw, tm)
    s_all, stats = _peer_select(xn, fw['wqt'], fw['subkeys'], 256)
    h2 = _peer_dense(xn, fw['u_b'], fw['vt_b'], s_all, stats, h1, tn, te)
    return _ple(h2, ple2d, fw, tm)


def _finish_weights(g_out_ssm, g_out_attn, w_out, g_ffn, peer_w_q, peer_subkeys, peer_u, peer_v, g_ple, ple_w_gate, ple_w_proj):
    perm = _q_perm()
    fw = {}
    fw['gos'] = g_out_ssm.reshape(1, SSM_WIDTH)
    fw['goa'] = g_out_attn[perm].reshape(1, ATTN_WIDTH)
    fw['wout_a'] = w_out[:SSM_WIDTH].astype(BF16)
    fw['wout_b'] = w_out[SSM_WIDTH:][perm].astype(BF16)
    fw['gffn'] = g_ffn.reshape(1, D_MODEL)
    fw['wqt'] = peer_w_q.T.astype(BF16)
    fw['subkeys'] = peer_subkeys.reshape(2 * PEER_HEADS, PEER_KEYS, -1).astype(BF16)
    fw['u_b'] = peer_u.astype(BF16)
    fw['vt_b'] = peer_v.T.astype(BF16)
    fw['gple'] = g_ple.reshape(1, D_MODEL)
    fw['wgate'] = ple_w_gate.astype(BF16)
    fw['wproj'] = ple_w_proj.astype(BF16)
    return fw


def _sample_attention(proj, page_table, pool_kc, pool_vc, pool_ks, pool_vs, win_k, win_v, cw, tabs_c, sw, gexp,
                      past_len, bsz):
    q, kc_new, vc_new, gate = proj[1], proj[2], proj[3], proj[12]
    ksb, vsb, kwb, vwb = proj[8], proj[9], proj[10], proj[11]
    n_pool = pool_kc.shape[0]
    per_b = lambda a, w: a.reshape(bsz, DEC_Q, w)
    pad_new = lambda a: jnp.pad(per_b(a, LANES), ((0, 0), (0, LANES - DEC_Q), (0, 0)))
    kt_view = lambda p: jnp.transpose(p, (0, 2, 3, 1))
    rows_view = lambda p: kt_view(p).reshape(n_pool, KV_WIDTH, PAGE)
    ocmp, sel = _sample_cmp(page_table, rows_view(pool_kc), rows_view(pool_vc), per_b(kc_new, LANES), per_b(vc_new, LANES),
                            per_b(q, 512), cw, tabs_c, sw, past_len)
    return _sample_sel(page_table, kt_view(pool_ks), kt_view(pool_vs), per_b(q, 512), pad_new(ksb), pad_new(vsb),
                       kt_view(win_k), kt_view(win_v), pad_new(kwb), pad_new(vwb), sel, ocmp, per_b(gate, LANES),
                       sw, gexp, past_len)


def _q_perm():
    idx = []
    for j in range(4):
        for half in range(2):
            h = j + 4 * half
            idx += [h * HEAD_DIM + d for d in range(HEAD_DIM)]
    return np.asarray(idx, np.int32)


def _ones_bd(width):
    i = np.arange(width)
    return jnp.asarray((i[:, None] // HEAD_DIM) == (i[None, :] // HEAD_DIM), BF16)


def _rope_tables(pos):
    half = ROPE_DIM // 2
    inv = jnp.power(ROPE_THETA, -jnp.arange(half, dtype=F32) / half)
    ang = pos.astype(F32)[:, None] * inv[None, :]
    cos, sin = jnp.cos(ang), jnp.sin(ang)
    n = pos.shape[0]
    ones = jnp.ones((n, HEAD_DIM - ROPE_DIM), F32)
    zeros = jnp.zeros((n, HEAD_DIM - ROPE_DIM), F32)
    z8 = jnp.zeros((n, half), F32)
    c = jnp.concatenate([cos, cos, ones], axis=1)
    s1 = jnp.concatenate([-sin, z8, zeros], axis=1)
    s2 = jnp.concatenate([z8, sin, zeros], axis=1)
    two = lambda a: jnp.concatenate([a, a], axis=1)
    return two(c), two(s1), two(s2)


def _prep_weights(g_mix, w_in, ssm_a_re, ssm_a_im, ssm_log_dt, ssm_b_re, ssm_b_im, ssm_c_re, ssm_c_im, ssm_d,
                  ssm_w_glu, ssm_b_glu, g_q, g_k_sel, g_k_win):
    pw = {}
    perm = _q_perm()
    w = w_in
    pw['gmix'] = g_mix.reshape(1, D_MODEL)
    pw['wu'] = w[:, 0:512].astype(BF16)
    pw['wq'] = w[:, 512:1024][:, perm].astype(BF16)
    pw['wk'] = w[:, 1024:1792].astype(BF16)
    pw['wg'] = jnp.pad(w[:, 1792:1816], ((0, 0), (0, LANES - 24))).astype(BF16)
    pw['gq'] = jnp.tile(g_q.reshape(1, HEAD_DIM), (1, 8))
    pw['gks'] = jnp.tile(g_k_sel.reshape(1, HEAD_DIM), (1, 2))
    pw['gkw'] = jnp.tile(g_k_win.reshape(1, HEAD_DIM), (1, 2))
    pw['ones512'] = _ones_bd(512)
    pw['ones128'] = _ones_bd(128)

    a_re, a_im = ssm_a_re.astype(F32), ssm_a_im.astype(F32)
    dt = jnp.exp(ssm_log_dt.astype(F32))[:, None]
    mag = jnp.exp(a_re * dt)
    ab_re = mag * jnp.cos(a_im * dt)
    ab_im = mag * jnp.sin(a_im * dt)
    den = a_re * a_re + a_im * a_im
    f_re = ((ab_re - 1.0) * a_re + ab_im * a_im) / den
    f_im = (ab_im * a_re - (ab_re - 1.0) * a_im) / den
    bb_re = f_re[..., None] * ssm_b_re - f_im[..., None] * ssm_b_im
    bb_im = f_re[..., None] * ssm_b_im + f_im[..., None] * ssm_b_re
    pw['ssm_ar'] = ab_re.reshape(N_PAIR, LANES)
    pw['ssm_ai'] = ab_im.reshape(N_PAIR, LANES)

    def in_block(bmat):
        m = bmat.reshape(4, 8, SSM_STATE, SSM_CH)
        eye = jnp.eye(8, dtype=F32)
        full = jnp.einsum('jgpc,gh->jgchp', m, eye)
        return full.reshape(4, 8 * SSM_CH, 8 * SSM_STATE)
    pw['ssm_wb'] = jnp.concatenate([in_block(bb_re), in_block(bb_im)], axis=2).astype(BF16)

    def out_block(cmat):
        m = cmat.reshape(4, 4, 2, SSM_CH, SSM_STATE)
        sel = np.zeros((4, 2, 8), np.float32)
        for pp in range(4):
            for e in range(2):
                sel[pp, e, 2 * pp + e] = 1.0
        full = jnp.einsum('jqecp,qeh->jqephc', m, jnp.asarray(sel))
        return full.reshape(N_PAIR, 2 * SSM_STATE, 8 * SSM_CH)
    pw['ssm_wcr'] = out_block(ssm_c_re.astype(F32)).astype(BF16)
    pw['ssm_wci'] = out_block(ssm_c_im.astype(F32)).astype(BF16)
    pw['ssm_d'] = ssm_d.reshape(1, SSM_WIDTH)
    pw['ssm_wglu'] = ssm_w_glu.astype(BF16)
    pw['ssm_bglu'] = ssm_b_glu.reshape(1, SSM_WIDTH)
    return pw


def kernel(x_prompt, x_sample, cache_k_cmp, cache_v_cmp, cache_k_sel, cache_v_sel, cache_k_win, cache_v_win, state_ssm_re, state_ssm_im, page_table, p_prompt, p_sample, g_mix, w_in, ssm_a_re, ssm_a_im, ssm_log_dt, ssm_b_re, ssm_b_im, ssm_c_re, ssm_c_im, ssm_d, ssm_w_glu, ssm_b_glu, g_q, g_k_cmp, g_k_sel, g_k_win, cmp_pe_k, cmp_w1_k, cmp_w2_k, cmp_pe_v, cmp_w1_v, cmp_w2_v, g_out_ssm, g_out_attn, w_out, g_ffn, peer_w_q, peer_subkeys, peer_u, peer_v, g_ple, ple_w_gate, ple_w_proj):
    bp, tp, _ = x_prompt.shape
    bs, ts, _ = x_sample.shape
    past_len = page_table.shape[1] * cache_k_cmp.shape[2]
    pw = _prep_weights(g_mix[0], w_in[0], ssm_a_re[0], ssm_a_im[0], ssm_log_dt[0], ssm_b_re[0], ssm_b_im[0],
                       ssm_c_re[0], ssm_c_im[0], ssm_d[0], ssm_w_glu[0], ssm_b_glu[0], g_q[0], g_k_sel[0], g_k_win[0])
    tabs_p = _rope_tables(jnp.arange(tp, dtype=jnp.int32))
    pos_s = past_len + jnp.arange(ts, dtype=jnp.int32)
    tabs_s = tuple(jnp.tile(a, (bs, 1)) for a in _rope_tables(pos_s))
    pp = _project(x_prompt.reshape(bp * tp, D_MODEL), tabs_p, tp // 512, 512, pw)
    ps = _project(x_sample.reshape(bs * ts, D_MODEL), tabs_s, 1, bs * ts, pw)
    zero_state = jnp.zeros((bp, N_PAIR, LANES), F32)
    yp, srp, sip = _ssm(pp[0].reshape(bp, tp, SSM_WIDTH), zero_state, zero_state, pw, bp, 128)
    ys, srs, sis = _ssm(ps[0].reshape(bs, ts, SSM_WIDTH), state_ssm_re[0].reshape(bs, N_PAIR, LANES),
                        state_ssm_im[0].reshape(bs, N_PAIR, LANES), pw, 8, ts)

    cw = {'gkc': jnp.tile(g_k_cmp[0].reshape(1, HEAD_DIM), (1, 2)), 'ones128': pw['ones128']}
    _compress_weights(cmp_pe_k[0], cmp_w1_k[0], cmp_w2_k[0], 'k', cw)
    _compress_weights(cmp_pe_v[0], cmp_w1_v[0], cmp_w2_v[0], 'v', cw)
    fw = _finish_weights(g_out_ssm[0], g_out_attn[0], w_out[0], g_ffn[0], peer_w_q[0], peer_subkeys[0], peer_u[0],
                         peer_v[0], g_ple[0], ple_w_gate[0], ple_w_proj[0])
    aw = _attn_consts(tp)
    n_pages = page_table.shape[1]
    sw = _sample_consts(n_pages)
    cmp_pos = lambda n: jnp.arange(n, dtype=jnp.int32) * CMP_STRIDE + (2 * CMP_STRIDE - 1)

    nblk = tp // CMP_STRIDE
    kcc, vcc = _compress_prompt(pp[2].reshape(bp, nblk, CMP_STRIDE * KV_WIDTH), pp[3].reshape(bp, nblk, CMP_STRIDE * KV_WIDTH),
                                cw, _rope_tables(cmp_pos(nblk)))
    att_p = _attn_prompt(pp[1], pp[12], kcc, vcc, pp[8], pp[9], pp[10], pp[11], aw, bp, tp, 256)
    hp = _finish(x_prompt.reshape(bp * tp, D_MODEL), yp.reshape(bp * tp, SSM_WIDTH), att_p,
                 p_prompt[0].reshape(bp * tp, -1), fw, 512, 512, 1024)

    ncp = n_pages * (PAGE // CMP_STRIDE) + 8
    att_s = _sample_attention(ps, page_table, cache_k_cmp[0], cache_v_cmp[0], cache_k_sel[0], cache_v_sel[0],
                              cache_k_win[0], cache_v_win[0], cw, _rope_tables(cmp_pos(ncp)), sw, aw['gexp'], past_len, bs)
    hs = _finish(x_sample.reshape(bs * ts, D_MODEL), ys.reshape(bs * ts, SSM_WIDTH), att_s.reshape(bs * ts, ATTN_WIDTH),
                 p_sample[0].reshape(bs * ts, -1), fw, bs * ts, bs * ts, 1024)

    rows = lambda a, b, t: a.reshape(1, b, t, N_KV, HEAD_DIM)
    state = lambda a, b: a.reshape(1, b, SSM_GROUPS, SSM_STATE)
    keep = min(WINDOW, tp)
    p_kw = rows(pp[6], bp, tp)[:, :, tp - keep:]
    p_vw = rows(pp[7], bp, tp)[:, :, tp - keep:]
    s_kw = jnp.concatenate([cache_k_win[0], rows(ps[6], bs, ts)[0]], axis=1)[None, :, -WINDOW:]
    s_vw = jnp.concatenate([cache_v_win[0], rows(ps[7], bs, ts)[0]], axis=1)[None, :, -WINDOW:]
    return (hp.reshape(bp, tp, D_MODEL), hs.reshape(bs, ts, D_MODEL),
            rows(pp[2], bp, tp), rows(pp[3], bp, tp), rows(pp[4], bp, tp), rows(pp[5], bp, tp), p_kw, p_vw,
            state(srp, bp), state(sip, bp),
            rows(ps[2], bs, ts), rows(ps[3], bs, ts), rows(ps[4], bs, ts), rows(ps[5], bs, ts), s_kw, s_vw,
            state(srs, bs), state(sis, bs))
```
